```python
import math
import jax, jax.numpy as jnp
from jax import lax
import numpy as np

D_MODEL = 1024
BATCH = 8
SEQ = 4096
DEPTH = 2

D_MIX = D_MODEL
N_MIXERS = 4
D_GROUP = D_MIX // N_MIXERS

MLA_HEADS = 4
MLA_Q_RANK = D_MODEL // 4
MLA_KV_RANK = D_MODEL // 8
MLA_NOPE = 64
MLA_ROPE = 32
MLA_V = D_GROUP // MLA_HEADS
ROPE_THETA = 10000.0

CONV_WIDTH = 3
CONV_CH = D_GROUP

POOL_WINDOWS = (2, 4, 8, 16)
POOL_GROUPS = len(POOL_WINDOWS)
POOL_CH = D_GROUP // POOL_GROUPS

SWA_HEADS = 4
SWA_KV_HEADS = 2
SWA_HEAD_DIM = D_GROUP // SWA_HEADS
SWA_WINDOW = 128

BLOCK = 128

D_FF = -(-8 * D_MODEL // (3 * 256)) * 256

RMS_EPS = 1e-6

IN_SPLITS = (MLA_Q_RANK, MLA_KV_RANK, MLA_ROPE,
             CONV_CH, CONV_CH, CONV_CH,
             D_GROUP,
             SWA_HEADS * SWA_HEAD_DIM,
             SWA_KV_HEADS * SWA_HEAD_DIM,
             SWA_KV_HEADS * SWA_HEAD_DIM)
D_IN = sum(IN_SPLITS)

kernel_name = "hybrid_parallel_mla_conv_pool_swa"


def _split_points():
    pts, acc = [], 0
    for w in IN_SPLITS[:-1]:
        acc += w
        pts.append(acc)
    return pts


def _alibi_slopes(n):
    return np.asarray([2.0 ** (-8.0 * (i + 1) / n) for i in range(n)], dtype=np.float32)


def rmsnorm(x, g):
    xf = x.astype(jnp.float32)
    y = xf * lax.rsqrt(jnp.mean(xf * xf, axis=-1, keepdims=True) + RMS_EPS)
    return (y * g.astype(jnp.float32)).astype(x.dtype)


def rope_tables(seq, dim, dtype):
    inv = 1.0 / (ROPE_THETA ** (jnp.arange(0, dim, 2, dtype=jnp.float32) / dim))
    ang = jnp.arange(seq, dtype=jnp.float32)[:, None] * inv[None, :]
    return jnp.cos(ang).astype(dtype), jnp.sin(ang).astype(dtype)


def apply_rope(x, cos, sin):
    x1, x2 = jnp.split(x, 2, axis=-1)
    c = cos[:, None, :]
    s = sin[:, None, :]
    return jnp.concatenate([x1 * c - x2 * s, x1 * s + x2 * c], axis=-1)


def mla_attention(c_q, c_kv, k_r, q_norm_g, kv_norm_g, w_uq, w_ukv, cos, sin):
    b, s, _ = c_q.shape
    dqk = MLA_NOPE + MLA_ROPE
    q = (rmsnorm(c_q, q_norm_g) @ w_uq).reshape(b, s, MLA_HEADS, dqk)
    q_nope, q_rot = q[..., :MLA_NOPE], q[..., MLA_NOPE:]
    q_rot = apply_rope(q_rot, cos, sin)
    kv = (rmsnorm(c_kv, kv_norm_g) @ w_ukv).reshape(b, s, MLA_HEADS, MLA_NOPE + MLA_V)
    k_nope, v = kv[..., :MLA_NOPE], kv[..., MLA_NOPE:]
    k_rot = apply_rope(k_r[:, :, None, :], cos, sin)
    k = jnp.concatenate([k_nope, jnp.broadcast_to(k_rot, (b, s, MLA_HEADS, MLA_ROPE))], axis=-1)
    q = jnp.concatenate([q_nope, q_rot], axis=-1) * (1.0 / math.sqrt(dqk))
    nb = s // BLOCK
    qb = q.reshape(b, nb, BLOCK, MLA_HEADS, dqk).transpose(1, 0, 2, 3, 4)
    key_pos = jnp.arange(s)

    def one_block(args):
        q_blk, i = args
        sc = jnp.einsum('bqhd,bkhd->bhqk', q_blk, k).astype(jnp.float32)
        q_pos = i * BLOCK + jnp.arange(BLOCK)
        causal = key_pos[None, :] <= q_pos[:, None]
        sc = jnp.where(causal[None, None], sc, -jnp.inf)
        p = jax.nn.softmax(sc, axis=-1).astype(v.dtype)
        return jnp.einsum('bhqk,bkhd->bqhd', p, v)

    out = lax.map(one_block, (qb, jnp.arange(nb)))
    return out.transpose(1, 0, 2, 3, 4).reshape(b, s, MLA_HEADS * MLA_V)


def short_gated_conv(gate_b, gate_c, u, conv_w):
    z = gate_c * u
    y = lax.conv_general_dilated(
        z, conv_w[:, None, :].astype(z.dtype), window_strides=(1,),
        padding=[(CONV_WIDTH - 1, 0)], dimension_numbers=('NWC', 'WIO', 'NWC'),
        feature_group_count=CONV_CH)
    return gate_b * y


def multiscale_pool(u, pool_w, pool_scale):
    b, s, _ = u.shape
    uf = u.astype(jnp.float32)
    cs = jnp.cumsum(uf, axis=1)
    pos = jnp.arange(s)
    outs = []
    for g, w in enumerate(POOL_WINDOWS):
        cs_g = cs[:, :, g * POOL_CH:(g + 1) * POOL_CH]
        lag = jnp.pad(cs_g, ((0, 0), (w, 0), (0, 0)))[:, :s]
        count = jnp.minimum(pos + 1, w).astype(jnp.float32)[None, :, None]
        outs.append((cs_g - lag) / count)
    pooled = jnp.stack(outs, axis=2) - uf.reshape(b, s, POOL_GROUPS, POOL_CH)
    mixed = jnp.einsum('bsgc,gcd->bsgd', pooled.astype(u.dtype), pool_w)
    return mixed.reshape(b, s, D_GROUP) * pool_scale


def swa_sink_attention(q, k, v, sinks, slopes):
    b, s, _, hd = q.shape
    grp = SWA_HEADS // SWA_KV_HEADS
    nb = s // BLOCK
    qb = q.reshape(b, nb, BLOCK, SWA_KV_HEADS, grp, hd)
    kb = k.reshape(b, nb, BLOCK, SWA_KV_HEADS, hd)
    vb = v.reshape(b, nb, BLOCK, SWA_KV_HEADS, hd)

    def with_prev(t):
        prev = jnp.pad(t, ((0, 0), (1, 0), (0, 0), (0, 0), (0, 0)))[:, :nb]
        return jnp.concatenate([prev, t], axis=2)

    kk, vv = with_prev(kb), with_prev(vb)
    sc = jnp.einsum('bnqkgd,bnskd->bnkgqs', qb, kk).astype(jnp.float32) * (1.0 / math.sqrt(hd))
    blk = jnp.arange(nb)[:, None] * BLOCK
    q_pos = blk + jnp.arange(BLOCK)[None, :]
    k_pos = blk - BLOCK + jnp.arange(2 * BLOCK)[None, :]
    dist = q_pos[:, :, None] - k_pos[:, None, :]
    valid = (dist >= 0) & (dist < SWA_WINDOW) & (k_pos[:, None, :] >= 0)
    sl = jnp.asarray(slopes).reshape(SWA_KV_HEADS, grp)
    bias = -sl[None, None, :, :, None, None] * dist.astype(jnp.float32)[None, :, None, None, :, :]
    sc = jnp.where(valid[None, :, None, None], sc + bias, -jnp.inf)
    sink = jnp.broadcast_to(
        sinks.astype(jnp.float32).reshape(SWA_KV_HEADS, grp)[None, None, :, :, None, None],
        sc.shape[:-1] + (1,))
    p = jax.nn.softmax(jnp.concatenate([sc, sink], axis=-1), axis=-1)[..., :-1].astype(v.dtype)
    out = jnp.einsum('bnkgqs,bnskd->bnqkgd', p, vv)
    return out.reshape(b, s, SWA_HEADS * hd)


def setup_inputs(seed: int = 0) -> dict:
    key = jax.random.key(seed)
    ks = jax.random.split(key, 17)
    f32 = jnp.float32

    def dense(k, shape, fan_in):
        return jax.random.normal(k, shape, f32) * fan_in ** -0.5

    def gain(k, shape):
        return 1.0 + 0.05 * jax.random.normal(k, shape, f32)

    return {
        "x": jax.random.normal(ks[0], (BATCH, SEQ, D_MODEL), f32),
        "attn_norm": gain(ks[1], (DEPTH, D_MODEL)),
        "w_in": dense(ks[2], (DEPTH, D_MODEL, D_IN), D_MODEL),
        "mla_q_norm": gain(ks[3], (DEPTH, MLA_Q_RANK)),
        "w_uq": dense(ks[4], (DEPTH, MLA_Q_RANK, MLA_HEADS * (MLA_NOPE + MLA_ROPE)), MLA_Q_RANK),
        "mla_kv_norm": gain(ks[5], (DEPTH, MLA_KV_RANK)),
        "w_ukv": dense(ks[6], (DEPTH, MLA_KV_RANK, MLA_HEADS * (MLA_NOPE + MLA_V)), MLA_KV_RANK),
        "conv_w": dense(ks[7], (DEPTH, CONV_WIDTH, CONV_CH), CONV_WIDTH),
        "pool_w": dense(ks[8], (DEPTH, POOL_GROUPS, POOL_CH, POOL_CH), POOL_CH),
        "pool_scale": gain(ks[9], (DEPTH, D_GROUP)),
        "swa_sinks": 0.5 * jax.random.normal(ks[10], (DEPTH, SWA_HEADS), f32),
        "mix_norm": gain(ks[11], (DEPTH, D_MIX)),
        "w_o": dense(ks[12], (DEPTH, D_MIX, D_MODEL), D_MIX),
        "ffn_norm": gain(ks[13], (DEPTH, D_MODEL)),
        "w_gate_up": dense(ks[14], (DEPTH, D_MODEL, 2 * D_FF), D_MODEL),
        "w_down": dense(ks[15], (DEPTH, D_FF, D_MODEL), D_FF),
        "final_norm": gain(ks[16], (D_MODEL,)),
    }


def reference(x, attn_norm, w_in, mla_q_norm, w_uq, mla_kv_norm, w_ukv, conv_w, pool_w,
              pool_scale, swa_sinks, mix_norm, w_o, ffn_norm, w_gate_up, w_down, final_norm):
    b, s, _ = x.shape
    cos, sin = rope_tables(s, MLA_ROPE, x.dtype)
    slopes = _alibi_slopes(SWA_HEADS)
    pts = _split_points()
    for l in range(DEPTH):
        h = rmsnorm(x, attn_norm[l])
        proj = h @ w_in[l]
        (c_q, c_kv, k_r, g_b, g_c, u_conv, u_pool,
         q_sw, k_sw, v_sw) = jnp.split(proj, pts, axis=-1)
        y_a = mla_attention(c_q, c_kv, k_r, mla_q_norm[l], mla_kv_norm[l],
                            w_uq[l], w_ukv[l], cos, sin)
        y_b = short_gated_conv(g_b, g_c, u_conv, conv_w[l])
        y_c = multiscale_pool(u_pool, pool_w[l], pool_scale[l])
        y_d = swa_sink_attention(q_sw.reshape(b, s, SWA_HEADS, SWA_HEAD_DIM),
                                 k_sw.reshape(b, s, SWA_KV_HEADS, SWA_HEAD_DIM),
                                 v_sw.reshape(b, s, SWA_KV_HEADS, SWA_HEAD_DIM),
                                 swa_sinks[l], slopes)
        groups = jnp.stack([y_a, y_b, y_c, y_d], axis=2)
        gf = groups.astype(jnp.float32)
        gf = gf * lax.rsqrt(jnp.mean(gf * gf, axis=-1, keepdims=True) + RMS_EPS)
        mixed = (gf.reshape(b, s, D_MIX) * mix_norm[l].astype(jnp.float32)).astype(x.dtype)
        x = x + mixed @ w_o[l]
        h2 = rmsnorm(x, ffn_norm[l])
        gate, up = jnp.split(h2 @ w_gate_up[l], 2, axis=-1)
        x = x + (jax.nn.silu(gate) * up) @ w_down[l]
    return rmsnorm(x, final_norm)
```

```python
import functools
import math

import jax
import jax.numpy as jnp
import numpy as np
from jax import lax
from jax.experimental import pallas as pl
from jax.experimental.pallas import tpu as pltpu

D_MODEL = 1024
D_GROUP = 256
MLA_HEADS = 4
MLA_Q_RANK = 256
MLA_KV_RANK = 128
MLA_NOPE = 64
MLA_ROPE = 32
MLA_V = 64
ROPE_THETA = 10000.0
CONV_WIDTH = 3
POOL_WINDOWS = (2, 4, 8, 16)
POOL_CH = 64
SWA_HEADS = 4
SWA_KV_HEADS = 2
SWA_HEAD_DIM = 64
SWA_WINDOW = 128
D_FF = 2816
RMS_EPS = 1e-6
D_IN = 1952

LANES = 128
V7X_VMEM_BYTES = 64 * 1024 * 1024

D_IN_PAD = 2048
OFF_CQ, OFF_CKV, OFF_GB, OFF_GC, OFF_UCONV, OFF_UPOOL = 0, 256, 384, 640, 896, 1152
OFF_QSW, OFF_KSW, OFF_VSW, OFF_KR = 1408, 1664, 1792, 1920
SLOT = LANES
MLA_W = MLA_HEADS * SLOT
POOL_HDR = 32
CONV_HDR = 8
NEG_BIG = -1e30

TM_IN = 512
TQ = 256
TK = 256
TM_OUT = 512
FF_CHUNK = 256
N_FF_CHUNKS = D_FF // FF_CHUNK


def _rms(x, g):
    return x * lax.rsqrt(jnp.mean(x * x, axis=-1, keepdims=True) + RMS_EPS) * g


def _dot(a, b):
    return lax.dot_general(a, b, (((1,), (0,)), ((), ())), preferred_element_type=jnp.float32)


def _dot_nt(a, b):
    return lax.dot_general(a, b, (((1,), (1,)), ((), ())), preferred_element_type=jnp.float32)


def _in_stage_kernel(x_ref, g_attn_ref, w_in_ref, g_q_ref, w_uq_ref, g_kv_ref, w_kv_ref,
                     tq_ref, tk_ref, conv_w_ref, w_pool_ref, pool_scale_ref, mixb_ref, mixc_ref,
                     q_ref, k_ref, v_ref, yb_ref, yc_ref, qsw_ref, ksw_ref, vsw_ref,
                     conv_scr, p0, p1, p2, *, tiles_per_seq):
    tm = x_ref.shape[0]
    t = pl.program_id(0)
    tile_in_seq = t % tiles_per_seq
    pos0 = pl.multiple_of(tile_in_seq * tm, tm)

    @pl.when(tile_in_seq == 0)
    def _():
        conv_scr[0:CONV_HDR, :] = jnp.zeros((CONV_HDR, D_GROUP), jnp.float32)
        p0[0:POOL_HDR, :] = jnp.zeros((POOL_HDR, D_GROUP), jnp.float32)

    h = _rms(x_ref[...], g_attn_ref[...]).astype(jnp.bfloat16)
    proj = _dot(h, w_in_ref[...])

    lane = lax.broadcasted_iota(jnp.int32, (tm, MLA_W), 1) % SLOT
    qn = _rms(proj[:, OFF_CQ:OFF_CQ + MLA_Q_RANK], g_q_ref[...]).astype(jnp.bfloat16)
    qa = _dot(qn, w_uq_ref[...])
    tq_tab = tq_ref[pl.ds(pos0, tm), :]
    qp = qa * jnp.concatenate([tq_tab] * MLA_HEADS, axis=1)
    q_rot = pltpu.roll(qp, MLA_W - MLA_ROPE, axis=1)
    q = jnp.where(lane < MLA_NOPE + MLA_ROPE, qp, 0.0) + jnp.where(
        (lane >= MLA_NOPE) & (lane < MLA_NOPE + MLA_ROPE), q_rot, 0.0)
    q_ref[...] = q.astype(jnp.bfloat16)

    ckn = _rms(proj[:, OFF_CKV:OFF_CKV + MLA_KV_RANK], g_kv_ref[...]).astype(jnp.bfloat16)
    kv = _dot(ckn, w_kv_ref[...])
    kr = proj[:, OFF_KR:OFF_KR + SLOT] * tk_ref[pl.ds(pos0, tm), :]
    lane1 = lax.broadcasted_iota(jnp.int32, (tm, SLOT), 1)
    kr = jnp.where((lane1 >= MLA_NOPE) & (lane1 < MLA_NOPE + MLA_ROPE),
                   kr + pltpu.roll(kr, SLOT - MLA_ROPE, axis=1), 0.0)
    k_ref[...] = (kv[:, :MLA_W] + jnp.concatenate([kr] * MLA_HEADS, axis=1)).astype(jnp.bfloat16)
    ones_col = jnp.where(lane == jnp.where((lax.broadcasted_iota(jnp.int32, (tm, MLA_W), 1) // SLOT) % 2 == 0,
                                           MLA_V, 0), 1.0, 0.0)
    v_ref[...] = (kv[:, MLA_W:] + ones_col).astype(jnp.bfloat16)

    z = proj[:, OFF_GC:OFF_GC + D_GROUP] * proj[:, OFF_UCONV:OFF_UCONV + D_GROUP]
    conv_scr[CONV_HDR:CONV_HDR + tm, :] = z
    z1 = conv_scr[CONV_HDR - 1:CONV_HDR - 1 + tm, :]
    z2 = conv_scr[CONV_HDR - 2:CONV_HDR - 2 + tm, :]
    cw = conv_w_ref[...]
    y_b = proj[:, OFF_GB:OFF_GB + D_GROUP] * (cw[0:1, :] * z2 + cw[1:2, :] * z1 + cw[2:3, :] * z)
    conv_scr[0:CONV_HDR, :] = z[tm - CONV_HDR:tm, :]
    yb_ref[...] = _rms(y_b, mixb_ref[...]).astype(jnp.bfloat16)

    u = proj[:, OFF_UPOOL:OFF_UPOOL + D_GROUP]
    n = tm + POOL_HDR
    p0[POOL_HDR:n, :] = u
    p1[8:n, :] = p0[8:n, :] + p0[7:n - 1, :]
    s2 = p1[POOL_HDR:n, :]
    p2[16:n, :] = p1[16:n, :] + p1[14:n - 2, :]
    s4 = p2[POOL_HDR:n, :]
    p1[24:n, :] = p2[24:n, :] + p2[20:n - 4, :]
    s8 = p1[POOL_HDR:n, :]
    s16 = s8 + p1[24:n - 8, :]
    p0[0:POOL_HDR, :] = u[tm - POOL_HDR:tm, :]
    lane_c = lax.broadcasted_iota(jnp.int32, (tm, D_GROUP), 1)
    row_c = lax.broadcasted_iota(jnp.int32, (tm, D_GROUP), 0)
    win = jnp.where(lane_c < POOL_CH, s2, jnp.where(lane_c < 2 * POOL_CH, s4,
                    jnp.where(lane_c < 3 * POOL_CH, s8, s16)))
    width = jnp.where(lane_c < POOL_CH, POOL_WINDOWS[0], jnp.where(lane_c < 2 * POOL_CH, POOL_WINDOWS[1],
                      jnp.where(lane_c < 3 * POOL_CH, POOL_WINDOWS[2], POOL_WINDOWS[3])))
    count = jnp.minimum(pos0 + row_c + 1, width).astype(jnp.float32)
    pooled = win / count - u
    y_c = _dot(pooled.astype(jnp.bfloat16), w_pool_ref[...]) * pool_scale_ref[...]
    yc_ref[...] = _rms(y_c, mixc_ref[...]).astype(jnp.bfloat16)

    qsw_ref[...] = (proj[:, OFF_QSW:OFF_QSW + D_GROUP] * (1.0 / math.sqrt(SWA_HEAD_DIM))).astype(jnp.bfloat16)
    lane_s = lax.broadcasted_iota(jnp.int32, (tm, SLOT), 1)
    for off, ref in ((OFF_KSW, ksw_ref), (OFF_VSW, vsw_ref)):
        a = proj[:, off:off + SLOT]
        r = pltpu.roll(a, SWA_HEAD_DIM, axis=1)
        ref[...] = jnp.concatenate([jnp.where(lane_s < SWA_HEAD_DIM, a, r),
                                    jnp.where(lane_s < SWA_HEAD_DIM, r, a)], axis=1).astype(jnp.bfloat16)


def _in_stage(x2d, lw, tabs, seq):
    n_tok = x2d.shape[0]
    tm = TM_IN
    const = lambda shape: pl.BlockSpec(shape, lambda t: (0, 0), pipeline_mode=pl.Buffered(1))
    tile = lambda w: pl.BlockSpec((tm, w), lambda t: (t, 0))
    bf = jnp.bfloat16
    out_shape = [jax.ShapeDtypeStruct((n_tok, w), bf) for w in (MLA_W, MLA_W, MLA_W, D_GROUP, D_GROUP,
                                                                 D_GROUP, D_GROUP, D_GROUP)]
    return pl.pallas_call(
        functools.partial(_in_stage_kernel, tiles_per_seq=seq // tm),
        grid=(n_tok // tm,),
        in_specs=[tile(D_MODEL), const((1, D_MODEL)), const((D_MODEL, D_IN_PAD)),
                  const((1, MLA_Q_RANK)), const((MLA_Q_RANK, MLA_W)),
                  const((1, MLA_KV_RANK)), const((MLA_KV_RANK, 2 * MLA_W)),
                  const((seq, SLOT)), const((seq, SLOT)),
                  const((CONV_WIDTH, D_GROUP)), const((D_GROUP, D_GROUP)), const((1, D_GROUP)),
                  const((1, D_GROUP)), const((1, D_GROUP))],
        out_specs=[tile(MLA_W), tile(MLA_W), tile(MLA_W), tile(D_GROUP), tile(D_GROUP),
                   tile(D_GROUP), tile(D_GROUP), tile(D_GROUP)],
        out_shape=out_shape,
        scratch_shapes=[pltpu.VMEM((tm + CONV_HDR, D_GROUP), jnp.float32),
                        pltpu.VMEM((tm + POOL_HDR, D_GROUP), jnp.float32),
                        pltpu.VMEM((tm + POOL_HDR, D_GROUP), jnp.float32),
                        pltpu.VMEM((tm + POOL_HDR, D_GROUP), jnp.float32)],
        compiler_params=pltpu.CompilerParams(dimension_semantics=("arbitrary",),
                                             vmem_limit_bytes=40 * 1024 * 1024),
        name="in_stage",
    )(x2d, lw["g_attn"], lw["w_in"], lw["g_q"], lw["w_uq"], lw["g_kv"], lw["w_kv"],
      tabs["tq"], tabs["tk"], lw["conv_w"], lw["w_pool"], lw["pool_scale"], lw["mix_b"], lw["mix_c"])


def _mla_kernel(q_ref, k_ref, v_ref, mix_ref, o_ref, m_scr, acc_scr):
    tq = q_ref.shape[0]
    i = pl.program_id(1)
    m_scr[...] = jnp.full(m_scr.shape, NEG_BIG, jnp.float32)
    acc_scr[...] = jnp.zeros(acc_scr.shape, jnp.float32)

    def update(h, rows, k_t, v_t, mask):
        sl = slice(h * SLOT, (h + 1) * SLOT)
        s = _dot_nt(q_ref[rows, sl], k_t[:, sl])
        if mask is not None:
            s = jnp.where(mask, s, NEG_BIG)
        m_old = m_scr[h, rows, :]
        m_new = jnp.maximum(m_old, jnp.max(s, axis=-1, keepdims=True))
        p = jnp.exp(s - m_new)
        acc_scr[h, rows, :] = jnp.exp(m_old - m_new) * acc_scr[h, rows, :] + _dot(p.astype(jnp.bfloat16), v_t[:, sl])
        m_scr[h, rows, :] = m_new

    def full_tile(j, carry):
        start = pl.multiple_of(j * TK, TK)
        k_t = k_ref[pl.ds(start, TK), :]
        v_t = v_ref[pl.ds(start, TK), :]
        for h in range(MLA_HEADS):
            update(h, slice(0, tq), k_t, v_t, None)
        return carry

    lax.fori_loop(0, i * (tq // TK), full_tile, 0)

    for d in range(tq // TK):
        start = pl.multiple_of(i * tq + d * TK, TK)
        k_t = k_ref[pl.ds(start, TK), :]
        v_t = v_ref[pl.ds(start, TK), :]
        nrow = tq - d * TK
        mask = (lax.broadcasted_iota(jnp.int32, (nrow, TK), 0) >= lax.broadcasted_iota(jnp.int32, (nrow, TK), 1))
        for h in range(MLA_HEADS):
            update(h, slice(d * TK, tq), k_t, v_t, mask)

    lane = lax.broadcasted_iota(jnp.int32, (tq, SLOT), 1)
    outs = []
    for g in range(MLA_HEADS // 2):
        a0 = acc_scr[2 * g]
        a1 = acc_scr[2 * g + 1]
        l0 = jnp.sum(jnp.where(lane == MLA_V, a0, 0.0), axis=-1, keepdims=True)
        l1 = jnp.sum(jnp.where(lane == 0, a1, 0.0), axis=-1, keepdims=True)
        outs.append(jnp.where(lane < MLA_V, a0 / l0, a1 / l1))
    y = jnp.concatenate(outs, axis=1)
    o_ref[...] = _rms(y, mix_ref[...]).astype(jnp.bfloat16)


def _mla_attention(q, k, v, mix_a, batch, seq):
    nq = seq // TQ
    return pl.pallas_call(
        _mla_kernel,
        grid=(batch, nq),
        in_specs=[pl.BlockSpec((TQ, MLA_W), lambda b, i: (b * nq + i, 0)),
                  pl.BlockSpec((seq, MLA_W), lambda b, i: (b, 0)),
                  pl.BlockSpec((seq, MLA_W), lambda b, i: (b, 0)),
                  pl.BlockSpec((1, D_GROUP), lambda b, i: (0, 0))],
        out_specs=pl.BlockSpec((TQ, D_GROUP), lambda b, i: (b * nq + i, 0)),
        out_shape=jax.ShapeDtypeStruct((batch * seq, D_GROUP), jnp.bfloat16),
        scratch_shapes=[pltpu.VMEM((MLA_HEADS, TQ, 1), jnp.float32),
                        pltpu.VMEM((MLA_HEADS, TQ, SLOT), jnp.float32)],
        compiler_params=pltpu.CompilerParams(dimension_semantics=("arbitrary", "arbitrary"),
                                             vmem_limit_bytes=40 * 1024 * 1024),
        name="mla_attention",
    )(q, k, v, mix_a)


def _swa_kernel(sinks_ref, q_ref, k_ref, v_ref, mix_ref, o_ref, *, slopes):
    tq = q_ref.shape[0]
    blk = SWA_WINDOW
    i = pl.program_id(1)
    lane_q = lax.broadcasted_iota(jnp.int32, (blk, D_GROUP), 1)
    row = lax.broadcasted_iota(jnp.int32, (blk, 2 * blk), 0)
    col = lax.broadcasted_iota(jnp.int32, (blk, 2 * blk), 1)
    lane_o = lax.broadcasted_iota(jnp.int32, (blk, SLOT), 1)
    for jb in range(tq // blk):
        q_pos0 = i * tq + jb * blk
        k_pos0 = jnp.maximum(q_pos0 - blk, 0)
        k_start = pl.multiple_of(k_pos0, blk)
        k_t = k_ref[pl.ds(k_start, 2 * blk), :]
        v_t = v_ref[pl.ds(k_start, 2 * blk), :]
        q_b = q_ref[jb * blk:(jb + 1) * blk, :]
        q_stack = jnp.concatenate(
            [jnp.where((lane_q >= h * SWA_HEAD_DIM) & (lane_q < (h + 1) * SWA_HEAD_DIM), q_b, 0)
             for h in range(SWA_HEADS)], axis=0)
        s_all = _dot_nt(q_stack, k_t)
        dist = (q_pos0 - k_pos0) + row - col
        valid = (dist >= 0) & (dist < SWA_WINDOW)
        dist_f = dist.astype(jnp.float32)
        p_list, inv_list = [], []
        for h in range(SWA_HEADS):
            s = s_all[h * blk:(h + 1) * blk, :] - slopes[h] * dist_f
            s = jnp.where(valid, s, NEG_BIG)
            sink = sinks_ref[h]
            m = jnp.maximum(jnp.max(s, axis=-1, keepdims=True), sink)
            p = jnp.exp(s - m)
            inv_list.append(1.0 / (jnp.sum(p, axis=-1, keepdims=True) + jnp.exp(sink - m)))
            p_list.append(p.astype(jnp.bfloat16))
        o_all = _dot(jnp.concatenate(p_list, axis=0), v_t)
        outs = []
        for g in range(SWA_KV_HEADS):
            h0, h1 = 2 * g, 2 * g + 1
            o0 = o_all[h0 * blk:(h0 + 1) * blk, g * SLOT:(g + 1) * SLOT] * inv_list[h0]
            o1 = o_all[h1 * blk:(h1 + 1) * blk, g * SLOT:(g + 1) * SLOT] * inv_list[h1]
            outs.append(jnp.where(lane_o < SWA_HEAD_DIM, o0, o1))
        y = jnp.concatenate(outs, axis=1)
        o_ref[jb * blk:(jb + 1) * blk, :] = _rms(y, mix_ref[...]).astype(jnp.bfloat16)


def _swa_attention(sinks, q, k, v, mix_d, batch, seq, slopes):
    tq = 512
    nq = seq // tq
    return pl.pallas_call(
        functools.partial(_swa_kernel, slopes=slopes),
        grid=(batch, nq),
        in_specs=[pl.BlockSpec(memory_space=pltpu.SMEM),
                  pl.BlockSpec((tq, D_GROUP), lambda b, i: (b * nq + i, 0)),
                  pl.BlockSpec((seq, D_GROUP), lambda b, i: (b, 0)),
                  pl.BlockSpec((seq, D_GROUP), lambda b, i: (b, 0)),
                  pl.BlockSpec((1, D_GROUP), lambda b, i: (0, 0))],
        out_specs=pl.BlockSpec((tq, D_GROUP), lambda b, i: (b * nq + i, 0)),
        out_shape=jax.ShapeDtypeStruct((batch * seq, D_GROUP), jnp.bfloat16),
        compiler_params=pltpu.CompilerParams(dimension_semantics=("arbitrary", "arbitrary"),
                                             vmem_limit_bytes=40 * 1024 * 1024),
        name="swa_attention",
    )(sinks, q, k, v, mix_d)


def _out_stage_kernel(x_ref, ya_ref, yb_ref, yc_ref, yd_ref, w_o_ref, g_ffn_ref, w_gate_ref, w_up_ref,
                      w_down_ref, g_final_ref, o_ref, act_scr, *, final):
    x = x_ref[...]
    for g, y_ref in enumerate((ya_ref, yb_ref, yc_ref, yd_ref)):
        x = x + _dot(y_ref[...], w_o_ref[g * D_GROUP:(g + 1) * D_GROUP, :])
    h2 = _rms(x, g_ffn_ref[...]).astype(jnp.bfloat16)

    def ff_chunk(c, carry):
        gate = _dot(h2, w_gate_ref[c])
        up = _dot(h2, w_up_ref[c])
        act_scr[c] = (gate * jax.nn.sigmoid(gate) * up).astype(jnp.bfloat16)
        return carry

    lax.fori_loop(0, N_FF_CHUNKS, ff_chunk, 0)
    for c in range(N_FF_CHUNKS):
        x = x + _dot(act_scr[c], w_down_ref[c])
    if final:
        x = _rms(x, g_final_ref[...])
    o_ref[...] = x


def _out_stage(x2d, ya, yb, yc, yd, lw, g_final, final):
    n_tok = x2d.shape[0]
    tm = TM_OUT
    const = lambda shape: pl.BlockSpec(shape, lambda t: (0,) * len(shape), pipeline_mode=pl.Buffered(1))
    tile = lambda w: pl.BlockSpec((tm, w), lambda t: (t, 0))
    return pl.pallas_call(
        functools.partial(_out_stage_kernel, final=final),
        grid=(n_tok // tm,),
        in_specs=[tile(D_MODEL), tile(D_GROUP), tile(D_GROUP), tile(D_GROUP), tile(D_GROUP),
                  const((D_MODEL, D_MODEL)), const((1, D_MODEL)), const((N_FF_CHUNKS, D_MODEL, FF_CHUNK)),
                  const((N_FF_CHUNKS, D_MODEL, FF_CHUNK)), const((N_FF_CHUNKS, FF_CHUNK, D_MODEL)),
                  const((1, D_MODEL))],
        out_specs=tile(D_MODEL),
        out_shape=jax.ShapeDtypeStruct((n_tok, D_MODEL), jnp.float32),
        scratch_shapes=[pltpu.VMEM((N_FF_CHUNKS, tm, FF_CHUNK), jnp.bfloat16)],
        compiler_params=pltpu.CompilerParams(dimension_semantics=("arbitrary",),
                                             vmem_limit_bytes=52 * 1024 * 1024),
        name="out_stage",
    )(x2d, ya, yb, yc, yd, lw["w_o"], lw["g_ffn"], lw["w_gate"], lw["w_up"], lw["w_down"], g_final)


def _rope_tables(seq):
    inv = 1.0 / (ROPE_THETA ** (jnp.arange(0, MLA_ROPE, 2, dtype=jnp.float32) / MLA_ROPE))
    ang = jnp.arange(seq, dtype=jnp.float32)[:, None] * inv[None, :]
    cos, sin = jnp.cos(ang), jnp.sin(ang)
    cos2 = jnp.concatenate([cos, cos], axis=1)
    sin2 = jnp.concatenate([sin, sin], axis=1)
    scale = 1.0 / math.sqrt(MLA_NOPE + MLA_ROPE)
    tq = jnp.concatenate([jnp.full((seq, MLA_NOPE), scale, jnp.float32), cos2 * scale, sin2 * scale], axis=1)
    tk = jnp.concatenate([jnp.zeros((seq, MLA_NOPE), jnp.float32), cos2, sin2], axis=1)
    return {"tq": tq, "tk": tk}


def _swap_halves(w):
    half = w.shape[-1] // 2
    return jnp.concatenate([-w[..., half:], w[..., :half]], axis=-1)


def _layer_weights(l, attn_norm, w_in, mla_q_norm, w_uq, mla_kv_norm, w_ukv, conv_w, pool_w, pool_scale,
                   mix_norm, w_o, ffn_norm, w_gate_up, w_down):
    bf = jnp.bfloat16
    f32 = jnp.float32
    wi = w_in[l]
    pts = np.cumsum((0, 256, 128, 32, 256, 256, 256, 256, 256, 128, 128))
    c_q, c_kv, k_r, g_b, g_c, u_conv, u_pool, q_sw, k_sw, v_sw = [wi[:, pts[j]:pts[j + 1]] for j in range(10)]
    zeros = lambda w: jnp.zeros((D_MODEL, w), f32)
    w_in_r = jnp.concatenate([c_q, c_kv, g_b, g_c, u_conv, u_pool, q_sw, k_sw, v_sw,
                              zeros(MLA_NOPE), k_r, _swap_halves(k_r)], axis=1)
    wq = w_uq[l].reshape(MLA_Q_RANK, MLA_HEADS, MLA_NOPE + MLA_ROPE)
    wq_rot = wq[..., MLA_NOPE:]
    w_uq_p = jnp.concatenate([wq, _swap_halves(wq_rot)], axis=-1).reshape(MLA_Q_RANK, MLA_W)
    wkv = w_ukv[l].reshape(MLA_KV_RANK, MLA_HEADS, MLA_NOPE + MLA_V)
    zk = jnp.zeros((MLA_KV_RANK, MLA_HEADS, SLOT - MLA_NOPE), f32)
    w_k = jnp.concatenate([wkv[..., :MLA_NOPE], zk], axis=-1).reshape(MLA_KV_RANK, MLA_W)
    wv = wkv[..., MLA_NOPE:]
    zv = jnp.zeros((MLA_KV_RANK, SLOT - MLA_V), f32)
    w_v = jnp.concatenate([jnp.concatenate([wv[:, h], zv], axis=-1) if h % 2 == 0
                           else jnp.concatenate([zv, wv[:, h]], axis=-1) for h in range(MLA_HEADS)], axis=-1)
    w_pool = jax.scipy.linalg.block_diag(*[pool_w[l, g] for g in range(len(POOL_WINDOWS))])
    mix = mix_norm[l].reshape(4, 1, D_GROUP)
    chunk_cols = lambda w: w.astype(bf).reshape(D_MODEL, N_FF_CHUNKS, FF_CHUNK).transpose(1, 0, 2)
    return {
        "g_attn": attn_norm[l][None, :], "w_in": w_in_r.astype(bf),
        "g_q": mla_q_norm[l][None, :], "w_uq": w_uq_p.astype(bf),
        "g_kv": mla_kv_norm[l][None, :], "w_kv": jnp.concatenate([w_k, w_v], axis=1).astype(bf),
        "conv_w": conv_w[l], "w_pool": w_pool.astype(bf), "pool_scale": pool_scale[l][None, :],
        "mix_a": mix[0], "mix_b": mix[1], "mix_c": mix[2], "mix_d": mix[3],
        "w_o": w_o[l].astype(bf), "g_ffn": ffn_norm[l][None, :],
        "w_gate": chunk_cols(w_gate_up[l][:, :D_FF]), "w_up": chunk_cols(w_gate_up[l][:, D_FF:]),
        "w_down": w_down[l].astype(bf).reshape(N_FF_CHUNKS, FF_CHUNK, D_MODEL),
    }


def kernel(x, attn_norm, w_in, mla_q_norm, w_uq, mla_kv_norm, w_ukv, conv_w, pool_w, pool_scale, swa_sinks,
           mix_norm, w_o, ffn_norm, w_gate_up, w_down, final_norm):
    batch, seq, d_model = x.shape
    depth = w_in.shape[0]
    assert d_model == D_MODEL and w_in.shape[2] == D_IN
    assert seq % TM_IN == 0 and seq % TQ == 0 and TQ % TK == 0 and (batch * seq) % TM_OUT == 0
    slopes = tuple(float(2.0 ** (-8.0 * (h + 1) / SWA_HEADS)) for h in range(SWA_HEADS))
    tabs = _rope_tables(seq)
    x2d = x.reshape(batch * seq, D_MODEL)
    g_final = final_norm[None, :]
    for l in range(depth):
        lw = _layer_weights(l, attn_norm, w_in, mla_q_norm, w_uq, mla_kv_norm, w_ukv, conv_w, pool_w,
                            pool_scale, mix_norm, w_o, ffn_norm, w_gate_up, w_down)
        q, k, v, yb, yc, qsw, ksw, vsw = _in_stage(x2d, lw, tabs, seq)
        ya = _mla_attention(q, k, v, lw["mix_a"], batch, seq)
        yd = _swa_attention(swa_sinks[l], qsw, ksw, vsw, lw["mix_d"], batch, seq, slopes)
        x2d = _out_stage(x2d, ya, yb, yc, yd, lw, g_final, final=(l == depth - 1))
    return x2d.reshape(batch, seq, D_MODEL)
```

```python
import functools
import math

import jax
import jax.numpy as jnp
import numpy as np
from jax import lax
from jax.experimental import pallas as pl
from jax.experimental.pallas import tpu as pltpu

D_MODEL = 1024
D_GROUP = 256
MLA_HEADS = 4
MLA_Q_RANK = 256
MLA_KV_RANK = 128
MLA_NOPE = 64
MLA_ROPE = 32
MLA_V = 64
ROPE_THETA = 10000.0
CONV_WIDTH = 3
POOL_WINDOWS = (2, 4, 8, 16)
POOL_CH = 64
SWA_HEADS = 4
SWA_KV_HEADS = 2
SWA_HEAD_DIM = 64
SWA_WINDOW = 128
D_FF = 2816
RMS_EPS = 1e-6
D_IN = 1952

LANES = 128
V7X_VMEM_BYTES = 64 * 1024 * 1024

D_IN_PAD = 2048
OFF_CQ, OFF_CKV, OFF_GB, OFF_GC, OFF_UCONV, OFF_UPOOL = 0, 256, 384, 640, 896, 1152
OFF_QSW, OFF_KSW, OFF_VSW, OFF_KR = 1408, 1664, 1792, 1920
SLOT = LANES
MLA_W = MLA_HEADS * SLOT
VT_ROWS = 80
VT_ALL = MLA_HEADS * VT_ROWS
VT_PAD = 384
LOG2E = math.log2(math.e)
POOL_HDR = 32
CONV_HDR = 8
NEG_BIG = -1e30

TM_IN = 512
TQ = 512
TK = 256
TM_OUT = 512
FF_CHUNK = 256
N_FF_CHUNKS = D_FF // FF_CHUNK


def _rms(x, g):
    return x * lax.rsqrt(jnp.mean(x * x, axis=-1, keepdims=True) + RMS_EPS) * g


def _dot(a, b):
    return lax.dot_general(a, b, (((1,), (0,)), ((), ())), preferred_element_type=jnp.float32)


def _dot_nt(a, b):
    return lax.dot_general(a, b, (((1,), (1,)), ((), ())), preferred_element_type=jnp.float32)


def _in_stage_kernel(x_ref, g_attn_ref, w_in_ref, g_q_ref, w_uq_ref, g_kv_ref, w_kv_ref,
                     tq_ref, tk_ref, conv_w_ref, w_pool_ref, pool_scale_ref, mixb_ref, mixc_ref,
                     q_ref, k_ref, vt_ref, yb_ref, yc_ref, qsw_ref, ksw_ref, vsw_ref,
                     conv_scr, p0, p1, p2, *, tiles_per_seq):
    tm = x_ref.shape[0]
    t = pl.program_id(0)
    tile_in_seq = t % tiles_per_seq
    pos0 = pl.multiple_of(tile_in_seq * tm, tm)

    @pl.when(tile_in_seq == 0)
    def _():
        conv_scr[0:CONV_HDR, :] = jnp.zeros((CONV_HDR, D_GROUP), jnp.float32)
        p0[0:POOL_HDR, :] = jnp.zeros((POOL_HDR, D_GROUP), jnp.float32)

    h = _rms(x_ref[...], g_attn_ref[...]).astype(jnp.bfloat16)
    proj = _dot(h, w_in_ref[...])

    lane = lax.broadcasted_iota(jnp.int32, (tm, MLA_W), 1) % SLOT
    qn = _rms(proj[:, OFF_CQ:OFF_CQ + MLA_Q_RANK], g_q_ref[...]).astype(jnp.bfloat16)
    qa = _dot(qn, w_uq_ref[...])
    tq_tab = tq_ref[pl.ds(pos0, tm), :]
    qp = qa * jnp.concatenate([tq_tab] * MLA_HEADS, axis=1)
    q_rot = pltpu.roll(qp, MLA_W - MLA_ROPE, axis=1)
    q = jnp.where(lane < MLA_NOPE + MLA_ROPE, qp, 0.0) + jnp.where(
        (lane >= MLA_NOPE) & (lane < MLA_NOPE + MLA_ROPE), q_rot, 0.0)
    q_ref[...] = q.astype(jnp.bfloat16)

    ckn = _rms(proj[:, OFF_CKV:OFF_CKV + MLA_KV_RANK], g_kv_ref[...]).astype(jnp.bfloat16)
    kv = _dot(ckn, w_kv_ref[...])
    kr = proj[:, OFF_KR:OFF_KR + SLOT] * tk_ref[pl.ds(pos0, tm), :]
    lane1 = lax.broadcasted_iota(jnp.int32, (tm, SLOT), 1)
    kr = jnp.where((lane1 >= MLA_NOPE) & (lane1 < MLA_NOPE + MLA_ROPE),
                   kr + pltpu.roll(kr, SLOT - MLA_ROPE, axis=1), 0.0)
    k_ref[...] = (kv[:, :MLA_W] + jnp.concatenate([kr] * MLA_HEADS, axis=1)).astype(jnp.bfloat16)
    lane_v = lax.broadcasted_iota(jnp.int32, (tm, VT_PAD), 1)
    ones_col = jnp.where((lane_v % VT_ROWS == MLA_V) & (lane_v < VT_ALL), 1.0, 0.0)
    vt_ref[...] = (kv[:, MLA_W:] + ones_col).T[:VT_ALL, :].astype(jnp.bfloat16)

    z = proj[:, OFF_GC:OFF_GC + D_GROUP] * proj[:, OFF_UCONV:OFF_UCONV + D_GROUP]
    conv_scr[CONV_HDR:CONV_HDR + tm, :] = z
    z1 = conv_scr[CONV_HDR - 1:CONV_HDR - 1 + tm, :]
    z2 = conv_scr[CONV_HDR - 2:CONV_HDR - 2 + tm, :]
    cw = conv_w_ref[...]
    y_b = proj[:, OFF_GB:OFF_GB + D_GROUP] * (cw[0:1, :] * z2 + cw[1:2, :] * z1 + cw[2:3, :] * z)
    conv_scr[0:CONV_HDR, :] = z[tm - CONV_HDR:tm, :]
    yb_ref[...] = _rms(y_b, mixb_ref[...]).astype(jnp.bfloat16)

    u = proj[:, OFF_UPOOL:OFF_UPOOL + D_GROUP]
    n = tm + POOL_HDR
    p0[POOL_HDR:n, :] = u
    p1[8:n, :] = p0[8:n, :] + p0[7:n - 1, :]
    s2 = p1[POOL_HDR:n, :]
    p2[16:n, :] = p1[16:n, :] + p1[14:n - 2, :]
    s4 = p2[POOL_HDR:n, :]
    p1[24:n, :] = p2[24:n, :] + p2[20:n - 4, :]
    s8 = p1[POOL_HDR:n, :]
    s16 = s8 + p1[24:n - 8, :]
    p0[0:POOL_HDR, :] = u[tm - POOL_HDR:tm, :]
    lane_c = lax.broadcasted_iota(jnp.int32, (tm, D_GROUP), 1)
    row_c = lax.broadcasted_iota(jnp.int32, (tm, D_GROUP), 0)
    win = jnp.where(lane_c < POOL_CH, s2, jnp.where(lane_c < 2 * POOL_CH, s4,
                    jnp.where(lane_c < 3 * POOL_CH, s8, s16)))
    width = jnp.where(lane_c < POOL_CH, POOL_WINDOWS[0], jnp.where(lane_c < 2 * POOL_CH, POOL_WINDOWS[1],
                      jnp.where(lane_c < 3 * POOL_CH, POOL_WINDOWS[2], POOL_WINDOWS[3])))
    count = jnp.minimum(pos0 + row_c + 1, width).astype(jnp.float32)
    pooled = win / count - u
    y_c = _dot(pooled.astype(jnp.bfloat16), w_pool_ref[...]) * pool_scale_ref[...]
    yc_ref[...] = _rms(y_c, mixc_ref[...]).astype(jnp.bfloat16)

    qsw_ref[...] = (proj[:, OFF_QSW:OFF_QSW + D_GROUP] * (1.0 / math.sqrt(SWA_HEAD_DIM))).astype(jnp.bfloat16)
    lane_s = lax.broadcasted_iota(jnp.int32, (tm, SLOT), 1)
    for off, ref in ((OFF_KSW, ksw_ref), (OFF_VSW, vsw_ref)):
        a = proj[:, off:off + SLOT]
        r = pltpu.roll(a, SWA_HEAD_DIM, axis=1)
        ref[...] = jnp.concatenate([jnp.where(lane_s < SWA_HEAD_DIM, a, r),
                                    jnp.where(lane_s < SWA_HEAD_DIM, r, a)], axis=1).astype(jnp.bfloat16)


def _in_stage(x2d, lw, tabs, seq):
    n_tok = x2d.shape[0]
    tm = TM_IN
    tiles_per_seq = seq // tm
    const = lambda shape: pl.BlockSpec(shape, lambda t: (0, 0), pipeline_mode=pl.Buffered(1))
    tile = lambda w: pl.BlockSpec((tm, w), lambda t: (t, 0))
    vt_spec = pl.BlockSpec((None, VT_ALL, tm), lambda t: (t // tiles_per_seq, 0, t % tiles_per_seq))
    bf = jnp.bfloat16
    tok = lambda w: jax.ShapeDtypeStruct((n_tok, w), bf)
    out_shape = [tok(MLA_W), tok(MLA_W), jax.ShapeDtypeStruct((n_tok // seq, VT_ALL, seq), bf),
                 tok(D_GROUP), tok(D_GROUP), tok(D_GROUP), tok(D_GROUP), tok(D_GROUP)]
    return pl.pallas_call(
        functools.partial(_in_stage_kernel, tiles_per_seq=tiles_per_seq),
        grid=(n_tok // tm,),
        in_specs=[tile(D_MODEL), const((1, D_MODEL)), const((D_MODEL, D_IN_PAD)),
                  const((1, MLA_Q_RANK)), const((MLA_Q_RANK, MLA_W)),
                  const((1, MLA_KV_RANK)), const((MLA_KV_RANK, MLA_W + VT_PAD)),
                  const((seq, SLOT)), const((seq, SLOT)),
                  const((CONV_WIDTH, D_GROUP)), const((D_GROUP, D_GROUP)), const((1, D_GROUP)),
                  const((1, D_GROUP)), const((1, D_GROUP))],
        out_specs=[tile(MLA_W), tile(MLA_W), vt_spec, tile(D_GROUP), tile(D_GROUP),
                   tile(D_GROUP), tile(D_GROUP), tile(D_GROUP)],
        out_shape=out_shape,
        scratch_shapes=[pltpu.VMEM((tm + CONV_HDR, D_GROUP), jnp.float32),
                        pltpu.VMEM((tm + POOL_HDR, D_GROUP), jnp.float32),
                        pltpu.VMEM((tm + POOL_HDR, D_GROUP), jnp.float32),
                        pltpu.VMEM((tm + POOL_HDR, D_GROUP), jnp.float32)],
        compiler_params=pltpu.CompilerParams(dimension_semantics=("arbitrary",),
                                             vmem_limit_bytes=40 * 1024 * 1024),
        name="in_stage",
    )(x2d, lw["g_attn"], lw["w_in"], lw["g_q"], lw["w_uq"], lw["g_kv"], lw["w_kv"],
      tabs["tq"], tabs["tk"], lw["conv_w"], lw["w_pool"], lw["pool_scale"], lw["mix_b"], lw["mix_c"])


def _mla_kernel(q_ref, k_ref, vt_ref, mix_ref, o_ref, acc_scr):
    tq = q_ref.shape[0]
    i = pl.program_id(1)
    acc_scr[...] = jnp.zeros(acc_scr.shape, jnp.float32)

    def scores(h, cols, key_start):
        k_h = k_ref[pl.ds(key_start, TK), h * SLOT:(h + 1) * SLOT]
        return _dot_nt(k_h, q_ref[cols, h * SLOT:(h + 1) * SLOT])

    def accumulate(h, s, m_old, cols, key_start, mask):
        if mask is not None:
            s = jnp.where(mask, s, NEG_BIG)
        m_new = jnp.maximum(m_old, jnp.max(s, axis=0, keepdims=True))
        p = jnp.exp2(s - m_new).astype(jnp.bfloat16)
        vt_h = vt_ref[h * VT_ROWS:(h + 1) * VT_ROWS, pl.ds(key_start, TK)]
        acc_scr[h, :, cols] = jnp.exp2(m_old - m_new) * acc_scr[h, :, cols] + _dot(vt_h, p)
        return m_new

    def run_units(units, m):
        m = list(m)
        s_next = scores(*units[0][:3])
        for u, (h, cols, key_start, mask) in enumerate(units):
            s_cur = s_next
            if u + 1 < len(units):
                s_next = scores(*units[u + 1][:3])
            ncol = cols.stop - cols.start
            m_new = accumulate(h, s_cur, m[h][:, m[h].shape[1] - ncol:], cols, key_start, mask)
            m[h] = m_new
        return tuple(m)

    all_cols = slice(0, tq)
    tiles_per_step = tq // TK

    def full_tiles(j, m):
        units = []
        for t in range(tiles_per_step):
            key_start = pl.multiple_of((j * tiles_per_step + t) * TK, TK)
            units += [(h, all_cols, key_start, None) for h in range(MLA_HEADS)]
        return run_units(units, m)

    m = tuple(jnp.full((1, tq), NEG_BIG, jnp.float32) for _ in range(MLA_HEADS))
    m = lax.fori_loop(0, i, full_tiles, m)

    units = []
    for d in range(tiles_per_step):
        key_start = pl.multiple_of(i * tq + d * TK, TK)
        ncol = tq - d * TK
        mask = (lax.broadcasted_iota(jnp.int32, (TK, ncol), 0) <= lax.broadcasted_iota(jnp.int32, (TK, ncol), 1))
        units += [(h, slice(d * TK, tq), key_start, mask) for h in range(MLA_HEADS)]
    run_units(units, m)

    y_t = jnp.concatenate([acc_scr[h, 0:MLA_V, :] / acc_scr[h, MLA_V:MLA_V + 1, :] for h in range(MLA_HEADS)], axis=0)
    o_ref[...] = _rms(y_t.T, mix_ref[...]).astype(jnp.bfloat16)


def _mla_attention(q, k, vt, mix_a, batch, seq):
    nq = seq // TQ
    return pl.pallas_call(
        _mla_kernel,
        grid=(batch, nq),
        in_specs=[pl.BlockSpec((TQ, MLA_W), lambda b, i: (b * nq + i, 0)),
                  pl.BlockSpec((seq, MLA_W), lambda b, i: (b, 0)),
                  pl.BlockSpec((None, VT_ALL, seq), lambda b, i: (b, 0, 0)),
                  pl.BlockSpec((1, D_GROUP), lambda b, i: (0, 0))],
        out_specs=pl.BlockSpec((TQ, D_GROUP), lambda b, i: (b * nq + i, 0)),
        out_shape=jax.ShapeDtypeStruct((batch * seq, D_GROUP), jnp.bfloat16),
        scratch_shapes=[pltpu.VMEM((MLA_HEADS, VT_ROWS, TQ), jnp.float32)],
        compiler_params=pltpu.CompilerParams(dimension_semantics=("arbitrary", "arbitrary"),
                                             vmem_limit_bytes=40 * 1024 * 1024),
        name="mla_attention",
    )(q, k, vt, mix_a)


def _swa_kernel(sinks_ref, q_ref, k_ref, v_ref, mix_ref, o_ref, *, slopes):
    tq = q_ref.shape[0]
    blk = SWA_WINDOW
    i = pl.program_id(1)
    lane_q = lax.broadcasted_iota(jnp.int32, (blk, D_GROUP), 1)
    row = lax.broadcasted_iota(jnp.int32, (blk, 2 * blk), 0)
    col = lax.broadcasted_iota(jnp.int32, (blk, 2 * blk), 1)
    lane_o = lax.broadcasted_iota(jnp.int32, (blk, SLOT), 1)
    for jb in range(tq // blk):
        q_pos0 = i * tq + jb * blk
        k_pos0 = jnp.maximum(q_pos0 - blk, 0)
        k_start = pl.multiple_of(k_pos0, blk)
        k_t = k_ref[pl.ds(k_start, 2 * blk), :]
        v_t = v_ref[pl.ds(k_start, 2 * blk), :]
        q_b = q_ref[jb * blk:(jb + 1) * blk, :]
        q_stack = jnp.concatenate(
            [jnp.where((lane_q >= h * SWA_HEAD_DIM) & (lane_q < (h + 1) * SWA_HEAD_DIM), q_b, 0)
             for h in range(SWA_HEADS)], axis=0)
        s_all = _dot_nt(q_stack, k_t)
        dist = (q_pos0 - k_pos0) + row - col
        valid = (dist >= 0) & (dist < SWA_WINDOW)
        dist_f = dist.astype(jnp.float32)
        p_list, inv_list = [], []
        for h in range(SWA_HEADS):
            s = s_all[h * blk:(h + 1) * blk, :] - slopes[h] * dist_f
            s = jnp.where(valid, s, NEG_BIG)
            sink = sinks_ref[h]
            m = jnp.maximum(jnp.max(s, axis=-1, keepdims=True), sink)
            p = jnp.exp(s - m)
            inv_list.append(1.0 / (jnp.sum(p, axis=-1, keepdims=True) + jnp.exp(sink - m)))
            p_list.append(p.astype(jnp.bfloat16))
        o_all = _dot(jnp.concatenate(p_list, axis=0), v_t)
        outs = []
        for g in range(SWA_KV_HEADS):
            h0, h1 = 2 * g, 2 * g + 1
            o0 = o_all[h0 * blk:(h0 + 1) * blk, g * SLOT:(g + 1) * SLOT] * inv_list[h0]
            o1 = o_all[h1 * blk:(h1 + 1) * blk, g * SLOT:(g + 1) * SLOT] * inv_list[h1]
            outs.append(jnp.where(lane_o < SWA_HEAD_DIM, o0, o1))
        y = jnp.concatenate(outs, axis=1)
        o_ref[jb * blk:(jb + 1) * blk, :] = _rms(y, mix_ref[...]).astype(jnp.bfloat16)


def _swa_attention(sinks, q, k, v, mix_d, batch, seq, slopes):
    tq = 512
    nq = seq // tq
    return pl.pallas_call(
        functools.partial(_swa_kernel, slopes=slopes),
        grid=(batch, nq),
        in_specs=[pl.BlockSpec(memory_space=pltpu.SMEM),
                  pl.BlockSpec((tq, D_GROUP), lambda b, i: (b * nq + i, 0)),
                  pl.BlockSpec((seq, D_GROUP), lambda b, i: (b, 0)),
                  pl.BlockSpec((seq, D_GROUP), lambda b, i: (b, 0)),
                  pl.BlockSpec((1, D_GROUP), lambda b, i: (0, 0))],
        out_specs=pl.BlockSpec((tq, D_GROUP), lambda b, i: (b * nq + i, 0)),
        out_shape=jax.ShapeDtypeStruct((batch * seq, D_GROUP), jnp.bfloat16),
        compiler_params=pltpu.CompilerParams(dimension_semantics=("arbitrary", "arbitrary"),
                                             vmem_limit_bytes=40 * 1024 * 1024),
        name="swa_attention",
    )(sinks, q, k, v, mix_d)


def _out_stage_kernel(x_ref, ya_ref, yb_ref, yc_ref, yd_ref, w_o_ref, g_ffn_ref, w_gate_ref, w_up_ref,
                      w_down_ref, g_final_ref, o_ref, act_scr, *, final):
    x = x_ref[...]
    for g, y_ref in enumerate((ya_ref, yb_ref, yc_ref, yd_ref)):
        x = x + _dot(y_ref[...], w_o_ref[g * D_GROUP:(g + 1) * D_GROUP, :])
    h2 = _rms(x, g_ffn_ref[...]).astype(jnp.bfloat16)

    def ff_chunk(c, carry):
        gate = _dot(h2, w_gate_ref[c])
        up = _dot(h2, w_up_ref[c])
        act_scr[c] = (gate * jax.nn.sigmoid(gate) * up).astype(jnp.bfloat16)
        return carry

    lax.fori_loop(0, N_FF_CHUNKS, ff_chunk, 0)
    for c in range(N_FF_CHUNKS):
        x = x + _dot(act_scr[c], w_down_ref[c])
    if final:
        x = _rms(x, g_final_ref[...])
    o_ref[...] = x


def _out_stage(x2d, ya, yb, yc, yd, lw, g_final, final):
    n_tok = x2d.shape[0]
    tm = TM_OUT
    const = lambda shape: pl.BlockSpec(shape, lambda t: (0,) * len(shape), pipeline_mode=pl.Buffered(1))
    tile = lambda w: pl.BlockSpec((tm, w), lambda t: (t, 0))
    return pl.pallas_call(
        functools.partial(_out_stage_kernel, final=final),
        grid=(n_tok // tm,),
        in_specs=[tile(D_MODEL), tile(D_GROUP), tile(D_GROUP), tile(D_GROUP), tile(D_GROUP),
                  const((D_MODEL, D_MODEL)), const((1, D_MODEL)), const((N_FF_CHUNKS, D_MODEL, FF_CHUNK)),
                  const((N_FF_CHUNKS, D_MODEL, FF_CHUNK)), const((N_FF_CHUNKS, FF_CHUNK, D_MODEL)),
                  const((1, D_MODEL))],
        out_specs=tile(D_MODEL),
        out_shape=jax.ShapeDtypeStruct((n_tok, D_MODEL), jnp.float32),
        scratch_shapes=[pltpu.VMEM((N_FF_CHUNKS, tm, FF_CHUNK), jnp.bfloat16)],
        compiler_params=pltpu.CompilerParams(dimension_semantics=("arbitrary",),
                                             vmem_limit_bytes=52 * 1024 * 1024),
        name="out_stage",
    )(x2d, ya, yb, yc, yd, lw["w_o"], lw["g_ffn"], lw["w_gate"], lw["w_up"], lw["w_down"], g_final)


def _rope_tables(seq):
    inv = 1.0 / (ROPE_THETA ** (jnp.arange(0, MLA_ROPE, 2, dtype=jnp.float32) / MLA_ROPE))
    ang = jnp.arange(seq, dtype=jnp.float32)[:, None] * inv[None, :]
    cos, sin = jnp.cos(ang), jnp.sin(ang)
    cos2 = jnp.concatenate([cos, cos], axis=1)
    sin2 = jnp.concatenate([sin, sin], axis=1)
    scale = LOG2E / math.sqrt(MLA_NOPE + MLA_ROPE)
    tq =jnp.concatenate([jnp.full((seq, MLA_NOPE), scale, jnp.float32), cos2 * scale, sin2 * scale], axis=1)
    tk = jnp.concatenate([jnp.zeros((seq, MLA_NOPE), jnp.float32), cos2, sin2], axis=1)
    return {"tq": tq, "tk": tk}


def _swap_halves(w):
    half = w.shape[-1] // 2
    return jnp.concatenate([-w[..., half:], w[..., :half]], axis=-1)


def _layer_weights(l, attn_norm, w_in, mla_q_norm, w_uq, mla_kv_norm, w_ukv, conv_w, pool_w, pool_scale,
                   mix_norm, w_o, ffn_norm, w_gate_up, w_down):
    bf = jnp.bfloat16
    f32 = jnp.float32
    wi = w_in[l]
    pts = np.cumsum((0, 256, 128, 32, 256, 256, 256, 256, 256, 128, 128))
    c_q, c_kv, k_r, g_b, g_c, u_conv, u_pool, q_sw, k_sw, v_sw = [wi[:, pts[j]:pts[j + 1]] for j in range(10)]
    zeros = lambda w: jnp.zeros((D_MODEL, w), f32)
    w_in_r = jnp.concatenate([c_q, c_kv, g_b, g_c, u_conv, u_pool, q_sw, k_sw, v_sw,
                              zeros(MLA_NOPE), k_r, _swap_halves(k_r)], axis=1)
    wq = w_uq[l].reshape(MLA_Q_RANK, MLA_HEADS, MLA_NOPE + MLA_ROPE)
    wq_rot = wq[..., MLA_NOPE:]
    w_uq_p = jnp.concatenate([wq, _swap_halves(wq_rot)], axis=-1).reshape(MLA_Q_RANK, MLA_W)
    wkv = w_ukv[l].reshape(MLA_KV_RANK, MLA_HEADS, MLA_NOPE + MLA_V)
    zk = jnp.zeros((MLA_KV_RANK, MLA_HEADS, SLOT - MLA_NOPE), f32)
    w_k = jnp.concatenate([wkv[..., :MLA_NOPE], zk], axis=-1).reshape(MLA_KV_RANK, MLA_W)
    zv = jnp.zeros((MLA_KV_RANK, MLA_HEADS, VT_ROWS - MLA_V), f32)
    w_v = jnp.concatenate([wkv[..., MLA_NOPE:], zv], axis=-1).reshape(MLA_KV_RANK, VT_ALL)
    w_v = jnp.concatenate([w_v, jnp.zeros((MLA_KV_RANK, VT_PAD - VT_ALL), f32)], axis=1)
    w_pool = jax.scipy.linalg.block_diag(*[pool_w[l, g] for g in range(len(POOL_WINDOWS))])
    mix = mix_norm[l].reshape(4, 1, D_GROUP)
    chunk_cols = lambda w: w.astype(bf).reshape(D_MODEL, N_FF_CHUNKS, FF_CHUNK).transpose(1, 0, 2)
    return {
        "g_attn": attn_norm[l][None, :], "w_in": w_in_r.astype(bf),
        "g_q": mla_q_norm[l][None, :], "w_uq": w_uq_p.astype(bf),
        "g_kv": mla_kv_norm[l][None, :], "w_kv": jnp.concatenate([w_k, w_v], axis=1).astype(bf),
        "conv_w": conv_w[l], "w_pool": w_pool.astype(bf), "pool_scale": pool_scale[l][None, :],
        "mix_a": mix[0], "mix_b": mix[1], "mix_c": mix[2], "mix_d": mix[3],
        "w_o": w_o[l].astype(bf), "g_ffn": ffn_norm[l][None, :],
        "w_gate": chunk_cols(w_gate_up[l][:, :D_FF]), "w_up": chunk_cols(w_gate_up[l][:, D_FF:]),
        "w_down": w_down[l].astype(bf).reshape(N_FF_CHUNKS, FF_CHUNK, D_MODEL),
    }


def kernel(x, attn_norm, w_in, mla_q_norm, w_uq, mla_kv_norm, w_ukv, conv_w, pool_w, pool_scale, swa_sinks,
           mix_norm, w_o, ffn_norm, w_gate_up, w_down, final_norm):
    batch, seq, d_model = x.shape
    depth = w_in.shape[0]
    assert d_model == D_MODEL and w_in.shape[2] == D_IN
    assert seq % TM_IN == 0 and seq % TQ == 0 and TQ % TK == 0 and (batch * seq) % TM_OUT == 0
    slopes = tuple(float(2.0 ** (-8.0 * (h + 1) / SWA_HEADS)) for h in range(SWA_HEADS))
    tabs = _rope_tables(seq)
    x2d = x.reshape(batch * seq, D_MODEL)
    g_final = final_norm[None, :]
    for l in range(depth):
        lw = _layer_weights(l, attn_norm, w_in, mla_q_norm, w_uq, mla_kv_norm, w_ukv, conv_w, pool_w,
                            pool_scale, mix_norm, w_o, ffn_norm, w_gate_up, w_down)
        q, k, vt, yb, yc, qsw, ksw, vsw = _in_stage(x2d, lw, tabs, seq)
        ya = _mla_attention(q, k, vt, lw["mix_a"], batch, seq)
        yd = _swa_attention(swa_sinks[l], qsw, ksw, vsw, lw["mix_d"], batch, seq, slopes)
        x2d = _out_stage(x2d, ya, yb, yc, yd, lw, g_final, final=(l == depth - 1))
    return x2d.reshape(batch, seq, D_MODEL)
```

```python
import functools
import math

import jax
import jax.numpy as jnp
import numpy as np
from jax import lax
from jax.experimental import pallas as pl
from jax.experimental.pallas import tpu as pltpu

D_MODEL = 1024
D_GROUP = 256
MLA_HEADS = 4
MLA_Q_RANK = 256
MLA_KV_RANK = 128
MLA_NOPE = 64
MLA_ROPE = 32
MLA_V = 64
ROPE_THETA = 10000.0
CONV_WIDTH = 3
POOL_WINDOWS = (2, 4, 8, 16)
POOL_CH = 64
SWA_HEADS = 4
SWA_KV_HEADS = 2
SWA_HEAD_DIM = 64
SWA_WINDOW = 128
D_FF = 2816
RMS_EPS = 1e-6
D_IN = 1952

LANES = 128
V7X_VMEM_BYTES = 64 * 1024 * 1024

D_IN_PAD = 2048
OFF_CQ, OFF_CKV, OFF_GB, OFF_GC, OFF_UCONV, OFF_UPOOL = 0, 256, 384, 640, 896, 1152
OFF_QSW, OFF_KSW, OFF_VSW, OFF_KR = 1408, 1664, 1792, 1920
SLOT = LANES
MLA_W = MLA_HEADS * SLOT
VT_ROWS = 80
VT_ALL = MLA_HEADS * VT_ROWS
VT_PAD = 384
LOG2E = math.log2(math.e)
POOL_HDR = 32
CONV_HDR = 8
NEG_BIG = -1e30

TM_IN = 1024
TQ = 512
TK = 256
TM_OUT = 512
FF_CHUNK = 256
N_FF_CHUNKS = D_FF // FF_CHUNK


def _rms(x, g):
    return x * lax.rsqrt(jnp.mean(x * x, axis=-1, keepdims=True) + RMS_EPS) * g


def _dot(a, b):
    return lax.dot_general(a, b, (((1,), (0,)), ((), ())), preferred_element_type=jnp.float32)


def _dot_nt(a, b):
    return lax.dot_general(a, b, (((1,), (1,)), ((), ())), preferred_element_type=jnp.float32)


def _in_stage_kernel(x_ref, g_attn_ref, w_in_ref, g_q_ref, w_uq_ref, g_kv_ref, w_kv_ref,
                     tq_ref, tk_ref, conv_w_ref, w_pool_ref, pool_scale_ref, mixb_ref, mixc_ref,
                     q_ref, k_ref, vt_ref, yb_ref, yc_ref, qsw_ref, ksw_ref, vswt_ref,
                     conv_scr, p0, p1, p2, *, tiles_per_seq):
    tm = x_ref.shape[0]
    t = pl.program_id(0)
    tile_in_seq = t % tiles_per_seq
    pos0 = pl.multiple_of(tile_in_seq * tm, tm)

    @pl.when(tile_in_seq == 0)
    def _():
        conv_scr[0:CONV_HDR, :] = jnp.zeros((CONV_HDR, D_GROUP), jnp.float32)
        p0[0:POOL_HDR, :] = jnp.zeros((POOL_HDR, D_GROUP), jnp.float32)

    h = _rms(x_ref[...], g_attn_ref[...]).astype(jnp.bfloat16)
    proj = _dot(h, w_in_ref[...])

    lane = lax.broadcasted_iota(jnp.int32, (tm, MLA_W), 1) % SLOT
    qn = _rms(proj[:, OFF_CQ:OFF_CQ + MLA_Q_RANK], g_q_ref[...]).astype(jnp.bfloat16)
    qa = _dot(qn, w_uq_ref[...])
    tq_tab = tq_ref[pl.ds(pos0, tm), :]
    qp = qa * jnp.concatenate([tq_tab] * MLA_HEADS, axis=1)
    q_rot = pltpu.roll(qp, MLA_W - MLA_ROPE, axis=1)
    q = jnp.where(lane < MLA_NOPE + MLA_ROPE, qp, 0.0) + jnp.where(
        (lane >= MLA_NOPE) & (lane < MLA_NOPE + MLA_ROPE), q_rot, 0.0)
    q_ref[...] = q.astype(jnp.bfloat16)

    ckn = _rms(proj[:, OFF_CKV:OFF_CKV + MLA_KV_RANK], g_kv_ref[...]).astype(jnp.bfloat16)
    kv = _dot(ckn, w_kv_ref[...])
    kr = proj[:, OFF_KR:OFF_KR + SLOT] * tk_ref[pl.ds(pos0, tm), :]
    lane1 = lax.broadcasted_iota(jnp.int32, (tm, SLOT), 1)
    kr = jnp.where((lane1 >= MLA_NOPE) & (lane1 < MLA_NOPE + MLA_ROPE),
                   kr + pltpu.roll(kr, SLOT - MLA_ROPE, axis=1), 0.0)
    k_ref[...] = (kv[:, :MLA_W] + jnp.concatenate([kr] * MLA_HEADS, axis=1)).astype(jnp.bfloat16)
    lane_v = lax.broadcasted_iota(jnp.int32, (tm, VT_PAD), 1)
    ones_col = jnp.where((lane_v % VT_ROWS == MLA_V) & (lane_v < VT_ALL), 1.0, 0.0)
    v_t = jnp.concatenate([proj[:, OFF_VSW:OFF_VSW + SLOT], kv[:, MLA_W:] + ones_col], axis=1).T
    vswt_ref[...] = v_t[:SLOT, :].astype(jnp.bfloat16)
    vt_ref[...] = v_t[SLOT:SLOT + VT_ALL, :].astype(jnp.bfloat16)

    z = proj[:, OFF_GC:OFF_GC + D_GROUP] * proj[:, OFF_UCONV:OFF_UCONV + D_GROUP]
    conv_scr[CONV_HDR:CONV_HDR + tm, :] = z
    z1 = conv_scr[CONV_HDR - 1:CONV_HDR - 1 + tm, :]
    z2 = conv_scr[CONV_HDR - 2:CONV_HDR - 2 + tm, :]
    cw = conv_w_ref[...]
    y_b = proj[:, OFF_GB:OFF_GB + D_GROUP] * (cw[0:1, :] * z2 + cw[1:2, :] * z1 + cw[2:3, :] * z)
    conv_scr[0:CONV_HDR, :] = z[tm - CONV_HDR:tm, :]
    yb_ref[...] = _rms(y_b, mixb_ref[...]).astype(jnp.bfloat16)

    u = proj[:, OFF_UPOOL:OFF_UPOOL + D_GROUP]
    n = tm + POOL_HDR
    p0[POOL_HDR:n, :] = u
    p1[8:n, :] = p0[8:n, :] + p0[7:n - 1, :]
    s2 = p1[POOL_HDR:n, :]
    p2[16:n, :] = p1[16:n, :] + p1[14:n - 2, :]
    s4 = p2[POOL_HDR:n, :]
    p1[24:n, :] = p2[24:n, :] + p2[20:n - 4, :]
    s8 = p1[POOL_HDR:n, :]
    s16 = s8 + p1[24:n - 8, :]
    p0[0:POOL_HDR, :] = u[tm - POOL_HDR:tm, :]
    lane_c = lax.broadcasted_iota(jnp.int32, (tm, D_GROUP), 1)
    row_c = lax.broadcasted_iota(jnp.int32, (tm, D_GROUP), 0)
    win = jnp.where(lane_c < POOL_CH, s2, jnp.where(lane_c < 2 * POOL_CH, s4,
                    jnp.where(lane_c < 3 * POOL_CH, s8, s16)))
    width = jnp.where(lane_c < POOL_CH, POOL_WINDOWS[0], jnp.where(lane_c < 2 * POOL_CH, POOL_WINDOWS[1],
                      jnp.where(lane_c < 3 * POOL_CH, POOL_WINDOWS[2], POOL_WINDOWS[3])))
    count = jnp.minimum(pos0 + row_c + 1, width).astype(jnp.float32)
    pooled = win / count - u
    y_c = _dot(pooled.astype(jnp.bfloat16), w_pool_ref[...]) * pool_scale_ref[...]
    yc_ref[...] = _rms(y_c, mixc_ref[...]).astype(jnp.bfloat16)

    qsw_ref[...] = (proj[:, OFF_QSW:OFF_QSW + D_GROUP] * (LOG2E / math.sqrt(SWA_HEAD_DIM))).astype(jnp.bfloat16)
    lane_s = lax.broadcasted_iota(jnp.int32, (tm, SLOT), 1)
    a = proj[:, OFF_KSW:OFF_KSW + SLOT]
    r = pltpu.roll(a, SWA_HEAD_DIM, axis=1)
    ksw_ref[...] = jnp.concatenate([jnp.where(lane_s < SWA_HEAD_DIM, a, r),
                                    jnp.where(lane_s < SWA_HEAD_DIM, r, a)], axis=1).astype(jnp.bfloat16)


def _in_stage(x2d, lw, tabs, seq):
    n_tok = x2d.shape[0]
    tm = TM_IN
    tiles_per_seq = seq // tm
    const = lambda shape: pl.BlockSpec(shape, lambda t: (0, 0), pipeline_mode=pl.Buffered(1))
    tile = lambda w: pl.BlockSpec((tm, w), lambda t: (t, 0))
    vt_spec = lambda rows: pl.BlockSpec((None, rows, tm), lambda t: (t // tiles_per_seq, 0, t % tiles_per_seq))
    bf = jnp.bfloat16
    tok = lambda w: jax.ShapeDtypeStruct((n_tok, w), bf)
    out_shape = [tok(MLA_W), tok(MLA_W), jax.ShapeDtypeStruct((n_tok // seq, VT_ALL, seq), bf),
                 tok(D_GROUP), tok(D_GROUP), tok(D_GROUP), tok(D_GROUP),
                 jax.ShapeDtypeStruct((n_tok // seq, SLOT, seq), bf)]
    return pl.pallas_call(
        functools.partial(_in_stage_kernel, tiles_per_seq=tiles_per_seq),
        grid=(n_tok // tm,),
        in_specs=[tile(D_MODEL), const((1, D_MODEL)), const((D_MODEL, D_IN_PAD)),
                  const((1, MLA_Q_RANK)), const((MLA_Q_RANK, MLA_W)),
                  const((1, MLA_KV_RANK)), const((MLA_KV_RANK, MLA_W + VT_PAD)),
                  const((seq, SLOT)), const((seq, SLOT)),
                  const((CONV_WIDTH, D_GROUP)), const((D_GROUP, D_GROUP)), const((1, D_GROUP)),
                  const((1, D_GROUP)), const((1, D_GROUP))],
        out_specs=[tile(MLA_W), tile(MLA_W), vt_spec(VT_ALL), tile(D_GROUP), tile(D_GROUP),
                   tile(D_GROUP), tile(D_GROUP), vt_spec(SLOT)],
        out_shape=out_shape,
        scratch_shapes=[pltpu.VMEM((tm + CONV_HDR, D_GROUP), jnp.float32),
                        pltpu.VMEM((tm + POOL_HDR, D_GROUP), jnp.float32),
                        pltpu.VMEM((tm + POOL_HDR, D_GROUP), jnp.float32),
                        pltpu.VMEM((tm + POOL_HDR, D_GROUP), jnp.float32)],
        compiler_params=pltpu.CompilerParams(dimension_semantics=("arbitrary",),
                                             vmem_limit_bytes=40 * 1024 * 1024),
        name="in_stage",
    )(x2d, lw["g_attn"], lw["w_in"], lw["g_q"], lw["w_uq"], lw["g_kv"], lw["w_kv"],
      tabs["tq"], tabs["tk"], lw["conv_w"], lw["w_pool"], lw["pool_scale"], lw["mix_b"], lw["mix_c"])


def _mla_kernel(q_ref, k_ref, vt_ref, mix_ref, o_ref, acc_scr):
    tq = q_ref.shape[0]
    i = pl.program_id(1)
    acc_scr[...] = jnp.zeros(acc_scr.shape, jnp.float32)

    def scores(h, cols, key_start):
        k_h = k_ref[pl.ds(key_start, TK), h * SLOT:(h + 1) * SLOT]
        return _dot_nt(k_h, q_ref[cols, h * SLOT:(h + 1) * SLOT])

    def accumulate(h, s, m_old, cols, key_start, mask):
        if mask is not None:
            s = jnp.where(mask, s, NEG_BIG)
        m_new = jnp.maximum(m_old, jnp.max(s, axis=0, keepdims=True))
        p = jnp.exp2(s - m_new).astype(jnp.bfloat16)
        vt_h = vt_ref[h * VT_ROWS:(h + 1) * VT_ROWS, pl.ds(key_start, TK)]
        acc_scr[h, :, cols] = jnp.exp2(m_old - m_new) * acc_scr[h, :, cols] + _dot(vt_h, p)
        return m_new

    def run_units(units, m):
        m = list(m)
        s_next = scores(*units[0][:3])
        for u, (h, cols, key_start, mask) in enumerate(units):
            s_cur = s_next
            if u + 1 < len(units):
                s_next = scores(*units[u + 1][:3])
            ncol = cols.stop - cols.start
            m_new = accumulate(h, s_cur, m[h][:, m[h].shape[1] - ncol:], cols, key_start, mask)
            m[h] = m_new
        return tuple(m)

    all_cols = slice(0, tq)
    tiles_per_step = tq // TK

    def full_tiles(j, m):
        units = []
        for t in range(tiles_per_step):
            key_start = pl.multiple_of((j * tiles_per_step + t) * TK, TK)
            units += [(h, all_cols, key_start, None) for h in range(MLA_HEADS)]
        return run_units(units, m)

    m = tuple(jnp.full((1, tq), NEG_BIG, jnp.float32) for _ in range(MLA_HEADS))
    m = lax.fori_loop(0, i, full_tiles, m)

    units = []
    for d in range(tiles_per_step):
        key_start = pl.multiple_of(i * tq + d * TK, TK)
        ncol = tq - d * TK
        mask = (lax.broadcasted_iota(jnp.int32, (TK, ncol), 0) <= lax.broadcasted_iota(jnp.int32, (TK, ncol), 1))
        units += [(h, slice(d * TK, tq), key_start, mask) for h in range(MLA_HEADS)]
    run_units(units, m)

    y_t = jnp.concatenate([acc_scr[h, 0:MLA_V, :] / acc_scr[h, MLA_V:MLA_V + 1, :] for h in range(MLA_HEADS)], axis=0)
    o_ref[...] = _rms(y_t.T, mix_ref[...]).astype(jnp.bfloat16)


def _mla_attention(q, k, vt, mix_a, batch, seq):
    nq = seq // TQ
    return pl.pallas_call(
        _mla_kernel,
        grid=(batch, nq),
        in_specs=[pl.BlockSpec((TQ, MLA_W), lambda b, i: (b * nq + i, 0)),
                  pl.BlockSpec((seq, MLA_W), lambda b, i: (b, 0)),
                  pl.BlockSpec((None, VT_ALL, seq), lambda b, i: (b, 0, 0)),
                  pl.BlockSpec((1, D_GROUP), lambda b, i: (0, 0))],
        out_specs=pl.BlockSpec((TQ, D_GROUP), lambda b, i: (b * nq + i, 0)),
        out_shape=jax.ShapeDtypeStruct((batch * seq, D_GROUP), jnp.bfloat16),
        scratch_shapes=[pltpu.VMEM((MLA_HEADS, VT_ROWS, TQ), jnp.float32)],
        compiler_params=pltpu.CompilerParams(dimension_semantics=("arbitrary", "arbitrary"),
                                             vmem_limit_bytes=40 * 1024 * 1024),
        name="mla_attention",
    )(q, k, vt, mix_a)


def _swa_kernel(sinks_ref, q_ref, k_ref, vt_ref, bias_ref, mix_ref, o_ref):
    tq = q_ref.shape[0]
    blk = SWA_WINDOW
    i = pl.program_id(1)
    lane_q = lax.broadcasted_iota(jnp.int32, (blk, D_GROUP), 1)
    head_of_col = lax.broadcasted_iota(jnp.int32, (1, SWA_HEADS * blk), 1) // blk
    sink_row = jnp.zeros((1, SWA_HEADS * blk), jnp.float32)
    for h in range(SWA_HEADS):
        sink_row = jnp.where(head_of_col == h, sinks_ref[h] * LOG2E, sink_row)
    def key_start(jb):
        return pl.multiple_of(jnp.maximum(i * tq + (jb - 1) * blk, 0), blk)

    def scores(jb):
        k_t = k_ref[pl.ds(key_start(jb), 2 * blk), :]
        q_b = q_ref[jb * blk:(jb + 1) * blk, :]
        q_stack = jnp.concatenate(
            [jnp.where((lane_q >= h * SWA_HEAD_DIM) & (lane_q < (h + 1) * SWA_HEAD_DIM), q_b, 0)
             for h in range(SWA_HEADS)], axis=0)
        return _dot_nt(k_t, q_stack)

    outs = []
    n_blk = tq // blk
    s_next = scores(0)
    for jb in range(n_blk):
        s_cur = s_next
        if jb + 1 < n_blk:
            s_next = scores(jb + 1)
        vt_t = vt_ref[:, pl.ds(key_start(jb), 2 * blk)]
        s = s_cur + bias_ref[jnp.minimum(i * tq + jb * blk, 1)]
        m = jnp.maximum(jnp.max(s, axis=0, keepdims=True), sink_row)
        p = jnp.exp2(s - m)
        inv = 1.0 / (jnp.sum(p, axis=0, keepdims=True) + jnp.exp2(sink_row - m))
        o = _dot(vt_t, p.astype(jnp.bfloat16)) * inv
        outs.append(jnp.concatenate(
            [o[(h // 2) * SWA_HEAD_DIM:(h // 2 + 1) * SWA_HEAD_DIM, h * blk:(h + 1) * blk]
             for h in range(SWA_HEADS)], axis=0))
    y_t = jnp.concatenate(outs, axis=1)
    o_ref[...] = _rms(y_t.T, mix_ref[...]).astype(jnp.bfloat16)


def _swa_bias_tables(slopes):
    blk = SWA_WINDOW
    key = np.arange(2 * blk)[:, None]
    qry = np.arange(blk)[None, :]
    tabs = []
    for off in (0, blk):
        dist = off + qry - key
        valid = (dist >= 0) & (dist < SWA_WINDOW)
        tabs.append(np.concatenate([np.where(valid, -s * LOG2E * dist, NEG_BIG) for s in slopes], axis=1))
    return jnp.asarray(np.stack(tabs), jnp.float32)


def _swa_attention(sinks, q, k, vt, bias, mix_d, batch, seq):
    tq = 512
    nq = seq // tq
    blk = SWA_WINDOW
    return pl.pallas_call(
        _swa_kernel,
        grid=(batch, nq),
        in_specs=[pl.BlockSpec(memory_space=pltpu.SMEM),
                  pl.BlockSpec((tq, D_GROUP), lambda b, i: (b * nq + i, 0)),
                  pl.BlockSpec((seq, D_GROUP), lambda b, i: (b, 0)),
                  pl.BlockSpec((None, SLOT, seq), lambda b, i: (b, 0, 0)),
                  pl.BlockSpec((2, 2 * blk, SWA_HEADS * blk), lambda b, i: (0, 0, 0)),
                  pl.BlockSpec((1, D_GROUP), lambda b, i: (0, 0))],
        out_specs=pl.BlockSpec((tq, D_GROUP), lambda b, i: (b * nq + i, 0)),
        out_shape=jax.ShapeDtypeStruct((batch * seq, D_GROUP), jnp.bfloat16),
        compiler_params=pltpu.CompilerParams(dimension_semantics=("arbitrary", "arbitrary"),
                                             vmem_limit_bytes=40 * 1024 * 1024),
        name="swa_attention",
    )(sinks, q, k, vt, bias, mix_d)


def _out_stage_kernel(x_ref, ya_ref, yb_ref, yc_ref, yd_ref, w_o_ref, g_ffn_ref, w_gu_ref,
                      w_down_ref, g_final_ref, o_ref, act_scr, *, final):
    x = x_ref[...]
    for g, y_ref in enumerate((ya_ref, yb_ref, yc_ref, yd_ref)):
        x = x + _dot(y_ref[...], w_o_ref[g * D_GROUP:(g + 1) * D_GROUP, :])
    h2 = _rms(x, g_ffn_ref[...]).astype(jnp.bfloat16)

    for c in range(N_FF_CHUNKS):
        cols = slice(c * FF_CHUNK, (c + 1) * FF_CHUNK)
        gate = _dot(h2, w_gu_ref[:, cols])
        up = _dot(h2, w_gu_ref[:, D_FF + c * FF_CHUNK:D_FF + (c + 1) * FF_CHUNK])
        act_scr[:, cols] = (gate * jax.nn.sigmoid(gate) * up).astype(jnp.bfloat16)
    x = x + _dot(act_scr[...], w_down_ref[...])
    if final:
        x = _rms(x, g_final_ref[...])
    o_ref[...] = x


def _out_stage(x2d, ya, yb, yc, yd, lw, g_final, final):
    n_tok = x2d.shape[0]
    tm = TM_OUT
    const = lambda shape: pl.BlockSpec(shape, lambda t: (0,) * len(shape), pipeline_mode=pl.Buffered(1))
    tile = lambda w: pl.BlockSpec((tm, w), lambda t: (t, 0))
    return pl.pallas_call(
        functools.partial(_out_stage_kernel, final=final),
        grid=(n_tok // tm,),
        in_specs=[tile(D_MODEL), tile(D_GROUP), tile(D_GROUP), tile(D_GROUP), tile(D_GROUP),
                  const((D_MODEL, D_MODEL)), const((1, D_MODEL)), const((D_MODEL, 2 * D_FF)),
                  const((D_FF, D_MODEL)), const((1, D_MODEL))],
        out_specs=tile(D_MODEL),
        out_shape=jax.ShapeDtypeStruct((n_tok, D_MODEL), jnp.float32),
        scratch_shapes=[pltpu.VMEM((tm, D_FF), jnp.bfloat16)],
        compiler_params=pltpu.CompilerParams(dimension_semantics=("arbitrary",),
                                             vmem_limit_bytes=52 * 1024 * 1024),
        name="out_stage",
    )(x2d, ya, yb, yc, yd, lw["w_o"], lw["g_ffn"], lw["w_gate_up"], lw["w_down"], g_final)


def _rope_tables(seq):
    inv = 1.0 / (ROPE_THETA ** (jnp.arange(0, MLA_ROPE, 2, dtype=jnp.float32) / MLA_ROPE))
    ang = jnp.arange(seq, dtype=jnp.float32)[:, None] * inv[None, :]
    cos, sin = jnp.cos(ang), jnp.sin(ang)
    cos2 = jnp.concatenate([cos, cos], axis=1)
    sin2 = jnp.concatenate([sin, sin], axis=1)
    scale = LOG2E / math.sqrt(MLA_NOPE + MLA_ROPE)
    tq =jnp.concatenate([jnp.full((seq, MLA_NOPE), scale, jnp.float32), cos2 * scale, sin2 * scale], axis=1)
    tk = jnp.concatenate([jnp.zeros((seq, MLA_NOPE), jnp.float32), cos2, sin2], axis=1)
    return {"tq": tq, "tk": tk}


def _swap_halves(w):
    half = w.shape[-1] // 2
    return jnp.concatenate([-w[..., half:], w[..., :half]], axis=-1)


def _layer_weights(l, attn_norm, w_in, mla_q_norm, w_uq, mla_kv_norm, w_ukv, conv_w, pool_w, pool_scale,
                   mix_norm, w_o, ffn_norm, w_gate_up, w_down):
    bf = jnp.bfloat16
    f32 = jnp.float32
    wi = w_in[l]
    pts = np.cumsum((0, 256, 128, 32, 256, 256, 256, 256, 256, 128, 128))
    c_q, c_kv, k_r, g_b, g_c, u_conv, u_pool, q_sw, k_sw, v_sw = [wi[:, pts[j]:pts[j + 1]] for j in range(10)]
    zeros = lambda w: jnp.zeros((D_MODEL, w), f32)
    w_in_r = jnp.concatenate([c_q, c_kv, g_b, g_c, u_conv, u_pool, q_sw, k_sw, v_sw,
                              zeros(MLA_NOPE), k_r, _swap_halves(k_r)], axis=1)
    wq = w_uq[l].reshape(MLA_Q_RANK, MLA_HEADS, MLA_NOPE + MLA_ROPE)
    wq_rot = wq[..., MLA_NOPE:]
    w_uq_p = jnp.concatenate([wq, _swap_halves(wq_rot)], axis=-1).reshape(MLA_Q_RANK, MLA_W)
    wkv = w_ukv[l].reshape(MLA_KV_RANK, MLA_HEADS, MLA_NOPE + MLA_V)
    zk = jnp.zeros((MLA_KV_RANK, MLA_HEADS, SLOT - MLA_NOPE), f32)
    w_k = jnp.concatenate([wkv[..., :MLA_NOPE], zk], axis=-1).reshape(MLA_KV_RANK, MLA_W)
    zv = jnp.zeros((MLA_KV_RANK, MLA_HEADS, VT_ROWS - MLA_V), f32)
    w_v = jnp.concatenate([wkv[..., MLA_NOPE:], zv], axis=-1).reshape(MLA_KV_RANK, VT_ALL)
    w_v = jnp.concatenate([w_v, jnp.zeros((MLA_KV_RANK, VT_PAD - VT_ALL), f32)], axis=1)
    w_pool = jax.scipy.linalg.block_diag(*[pool_w[l, g] for g in range(len(POOL_WINDOWS))])
    mix = mix_norm[l].reshape(4, 1, D_GROUP)
    return {
        "g_attn": attn_norm[l][None, :], "w_in": w_in_r.astype(bf),
        "g_q": mla_q_norm[l][None, :], "w_uq": w_uq_p.astype(bf),
        "g_kv": mla_kv_norm[l][None, :], "w_kv": jnp.concatenate([w_k, w_v], axis=1).astype(bf),
        "conv_w": conv_w[l], "w_pool": w_pool.astype(bf), "pool_scale": pool_scale[l][None, :],
        "mix_a": mix[0], "mix_b": mix[1], "mix_c": mix[2], "mix_d": mix[3],
        "w_o": w_o[l].astype(bf), "g_ffn": ffn_norm[l][None, :],
        "w_gate_up": w_gate_up[l].astype(bf), "w_down": w_down[l].astype(bf),
    }


def kernel(x, attn_norm, w_in, mla_q_norm, w_uq, mla_kv_norm, w_ukv, conv_w, pool_w, pool_scale, swa_sinks,
           mix_norm, w_o, ffn_norm, w_gate_up, w_down, final_norm):
    batch, seq, d_model = x.shape
    depth = w_in.shape[0]
    assert d_model == D_MODEL and w_in.shape[2] == D_IN
    assert seq % TM_IN == 0 and seq % TQ == 0 and TQ % TK == 0 and (batch * seq) % TM_OUT == 0
    slopes = tuple(float(2.0 ** (-8.0 * (h + 1) / SWA_HEADS)) for h in range(SWA_HEADS))
    tabs = _rope_tables(seq)
    swa_bias = _swa_bias_tables(slopes)
    x2d = x.reshape(batch * seq, D_MODEL)
    g_final = final_norm[None, :]
    for l in range(depth):
        lw = _layer_weights(l, attn_norm, w_in, mla_q_norm, w_uq, mla_kv_norm, w_ukv, conv_w, pool_w,
                            pool_scale, mix_norm, w_o, ffn_norm, w_gate_up, w_down)
        q, k, vt, yb, yc, qsw, ksw, vswt = _in_stage(x2d, lw, tabs, seq)
        ya = _mla_attention(q, k, vt, lw["mix_a"], batch, seq)
        yd = _swa_attention(swa_sinks[l], qsw, ksw, vswt, swa_bias, lw["mix_d"], batch, seq)
        x2d = _out_stage(x2d, ya, yb, yc, yd, lw, g_final, final=(l == depth - 1))
    return x2d.reshape(batch, seq, D_MODEL)
```

```python
import functools
import math

import jax
import jax.numpy as jnp
import numpy as np
from jax import lax
from jax.experimental import pallas as pl
from jax.experimental.pallas import tpu as pltpu

D_MODEL = 1024
D_GROUP = 256
MLA_HEADS = 4
MLA_Q_RANK = 256
MLA_KV_RANK = 128
MLA_NOPE = 64
MLA_ROPE = 32
MLA_V = 64
ROPE_THETA = 10000.0
CONV_WIDTH = 3
POOL_WINDOWS = (2, 4, 8, 16)
POOL_CH = 64
SWA_HEADS = 4
SWA_KV_HEADS = 2
SWA_HEAD_DIM = 64
SWA_WINDOW = 128
D_FF = 2816
RMS_EPS = 1e-6
D_IN = 1952

LANES = 128
V7X_VMEM_BYTES = 64 * 1024 * 1024

D_IN_PAD = 2048
OFF_CQ, OFF_CKV, OFF_GB, OFF_GC, OFF_UCONV, OFF_UPOOL = 0, 256, 384, 640, 896, 1152
OFF_QSW, OFF_KSW, OFF_VSW, OFF_KR = 1408, 1664, 1792, 1920
SLOT = LANES
MLA_W = MLA_HEADS * SLOT
VT_ROWS = 80
VT_ALL = MLA_HEADS * VT_ROWS
VT_PAD = 384
LOG2E = math.log2(math.e)
POOL_HDR = 32
CONV_HDR = 8
NEG_BIG = -1e30

TM_IN = 1024
TQ = 512
TK = 256
TM_OUT = 512
FF_CHUNK = 256
N_FF_CHUNKS = D_FF // FF_CHUNK


def _rms(x, g):
    return x * lax.rsqrt(jnp.mean(x * x, axis=-1, keepdims=True) + RMS_EPS) * g


def _dot(a, b):
    return lax.dot_general(a, b, (((1,), (0,)), ((), ())), preferred_element_type=jnp.float32)


def _zero_after(x):
    bits = pltpu.bitcast(x, jnp.int32)
    return lax.shift_right_logical(lax.shift_right_logical(bits, 16), 16).astype(jnp.float32)


def _dot_nt(a, b):
    return lax.dot_general(a, b, (((1,), (1,)), ((), ())), preferred_element_type=jnp.float32)


def _in_stage_kernel(x_ref, g_attn_ref, w_in_ref, g_q_ref, w_uq_ref, g_kv_ref, w_kv_ref,
                     tq_ref, tk_ref, conv_w_ref, w_pool_ref, pool_scale_ref, mixb_ref, mixc_ref,
                     q_ref, k_ref, vt_ref, yb_ref, yc_ref, qsw_ref, ksw_ref, vswt_ref,
                     conv_scr, p0, p1, p2, *, tiles_per_seq):
    tm = x_ref.shape[0]
    t = pl.program_id(0)
    tile_in_seq = t % tiles_per_seq
    pos0 = pl.multiple_of(tile_in_seq * tm, tm)

    @pl.when(tile_in_seq == 0)
    def _():
        conv_scr[0:CONV_HDR, :] = jnp.zeros((CONV_HDR, D_GROUP), jnp.float32)
        p0[0:POOL_HDR, :] = jnp.zeros((POOL_HDR, D_GROUP), jnp.float32)

    h = _rms(x_ref[...], g_attn_ref[...]).astype(jnp.bfloat16)
    proj = _dot(h, w_in_ref[...])

    lane = lax.broadcasted_iota(jnp.int32, (tm, MLA_W), 1) % SLOT
    qn = _rms(proj[:, OFF_CQ:OFF_CQ + MLA_Q_RANK], g_q_ref[...]).astype(jnp.bfloat16)
    qa = _dot(qn, w_uq_ref[...])
    tq_tab = tq_ref[pl.ds(pos0, tm), :]
    qp = qa * jnp.concatenate([tq_tab] * MLA_HEADS, axis=1)
    q_rot = pltpu.roll(qp, MLA_W - MLA_ROPE, axis=1)
    q = jnp.where(lane < MLA_NOPE + MLA_ROPE, qp, 0.0) + jnp.where(
        (lane >= MLA_NOPE) & (lane < MLA_NOPE + MLA_ROPE), q_rot, 0.0)
    q_ref[...] = q.astype(jnp.bfloat16)

    ckn = _rms(proj[:, OFF_CKV:OFF_CKV + MLA_KV_RANK], g_kv_ref[...]).astype(jnp.bfloat16)
    kv = _dot(ckn, w_kv_ref[...])
    kr = proj[:, OFF_KR:OFF_KR + SLOT] * tk_ref[pl.ds(pos0, tm), :]
    lane1 = lax.broadcasted_iota(jnp.int32, (tm, SLOT), 1)
    kr = jnp.where((lane1 >= MLA_NOPE) & (lane1 < MLA_NOPE + MLA_ROPE),
                   kr + pltpu.roll(kr, SLOT - MLA_ROPE, axis=1), 0.0)
    k_ref[...] = (kv[:, :MLA_W] + jnp.concatenate([kr] * MLA_HEADS, axis=1)).astype(jnp.bfloat16)
    lane_v = lax.broadcasted_iota(jnp.int32, (tm, VT_PAD), 1)
    ones_col = jnp.where((lane_v % VT_ROWS == MLA_V) & (lane_v < VT_ALL), 1.0, 0.0)
    v_t = jnp.concatenate([proj[:, OFF_VSW:OFF_VSW + SLOT], kv[:, MLA_W:] + ones_col], axis=1).T
    vswt_ref[...] = v_t[:SLOT, :].astype(jnp.bfloat16)
    vt_ref[...] = v_t[SLOT:SLOT + VT_ALL, :].astype(jnp.bfloat16)

    z = proj[:, OFF_GC:OFF_GC + D_GROUP] * proj[:, OFF_UCONV:OFF_UCONV + D_GROUP]
    conv_scr[CONV_HDR:CONV_HDR + tm, :] = z
    z1 = conv_scr[CONV_HDR - 1:CONV_HDR - 1 + tm, :]
    z2 = conv_scr[CONV_HDR - 2:CONV_HDR - 2 + tm, :]
    cw = conv_w_ref[...]
    y_b = proj[:, OFF_GB:OFF_GB + D_GROUP] * (cw[0:1, :] * z2 + cw[1:2, :] * z1 + cw[2:3, :] * z)
    conv_scr[0:CONV_HDR, :] = z[tm - CONV_HDR:tm, :]
    yb_ref[...] = _rms(y_b, mixb_ref[...]).astype(jnp.bfloat16)

    u = proj[:, OFF_UPOOL:OFF_UPOOL + D_GROUP]
    n = tm + POOL_HDR
    p0[POOL_HDR:n, :] = u
    p1[8:n, :] = p0[8:n, :] + p0[7:n - 1, :]
    s2 = p1[POOL_HDR:n, :]
    p2[16:n, :] = p1[16:n, :] + p1[14:n - 2, :]
    s4 = p2[POOL_HDR:n, :]
    p1[24:n, :] = p2[24:n, :] + p2[20:n - 4, :]
    s8 = p1[POOL_HDR:n, :]
    s16 = s8 + p1[24:n - 8, :]
    p0[0:POOL_HDR, :] = u[tm - POOL_HDR:tm, :]
    lane_c = lax.broadcasted_iota(jnp.int32, (tm, D_GROUP), 1)
    row_c = lax.broadcasted_iota(jnp.int32, (tm, D_GROUP), 0)
    win = jnp.where(lane_c < POOL_CH, s2, jnp.where(lane_c < 2 * POOL_CH, s4,
                    jnp.where(lane_c < 3 * POOL_CH, s8, s16)))
    width = jnp.where(lane_c < POOL_CH, POOL_WINDOWS[0], jnp.where(lane_c < 2 * POOL_CH, POOL_WINDOWS[1],
                      jnp.where(lane_c < 3 * POOL_CH, POOL_WINDOWS[2], POOL_WINDOWS[3])))
    count = jnp.minimum(pos0 + row_c + 1, width).astype(jnp.float32)
    pooled = win / count - u
    y_c = _dot(pooled.astype(jnp.bfloat16), w_pool_ref[...]) * pool_scale_ref[...]
    yc_ref[...] = _rms(y_c, mixc_ref[...]).astype(jnp.bfloat16)

    qsw_ref[...] = (proj[:, OFF_QSW:OFF_QSW + D_GROUP] * (LOG2E / math.sqrt(SWA_HEAD_DIM))).astype(jnp.bfloat16)
    lane_s = lax.broadcasted_iota(jnp.int32, (tm, SLOT), 1)
    a = proj[:, OFF_KSW:OFF_KSW + SLOT]
    r = pltpu.roll(a, SWA_HEAD_DIM, axis=1)
    ksw_ref[...] = jnp.concatenate([jnp.where(lane_s < SWA_HEAD_DIM, a, r),
                                    jnp.where(lane_s < SWA_HEAD_DIM, r, a)], axis=1).astype(jnp.bfloat16)


def _in_stage(x2d, lw, tabs, seq):
    n_tok = x2d.shape[0]
    tm = TM_IN
    tiles_per_seq = seq // tm
    const = lambda shape: pl.BlockSpec(shape, lambda t: (0, 0), pipeline_mode=pl.Buffered(1))
    tile = lambda w: pl.BlockSpec((tm, w), lambda t: (t, 0))
    vt_spec = lambda rows: pl.BlockSpec((None, rows, tm), lambda t: (t // tiles_per_seq, 0, t % tiles_per_seq))
    bf = jnp.bfloat16
    tok = lambda w: jax.ShapeDtypeStruct((n_tok, w), bf)
    out_shape = [tok(MLA_W), tok(MLA_W), jax.ShapeDtypeStruct((n_tok // seq, VT_ALL, seq), bf),
                 tok(D_GROUP), tok(D_GROUP), tok(D_GROUP), tok(D_GROUP),
                 jax.ShapeDtypeStruct((n_tok // seq, SLOT, seq), bf)]
    return pl.pallas_call(
        functools.partial(_in_stage_kernel, tiles_per_seq=tiles_per_seq),
        grid=(n_tok // tm,),
        in_specs=[tile(D_MODEL), const((1, D_MODEL)), const((D_MODEL, D_IN_PAD)),
                  const((1, MLA_Q_RANK)), const((MLA_Q_RANK, MLA_W)),
                  const((1, MLA_KV_RANK)), const((MLA_KV_RANK, MLA_W + VT_PAD)),
                  const((seq, SLOT)), const((seq, SLOT)),
                  const((CONV_WIDTH, D_GROUP)), const((D_GROUP, D_GROUP)), const((1, D_GROUP)),
                  const((1, D_GROUP)), const((1, D_GROUP))],
        out_specs=[tile(MLA_W), tile(MLA_W), vt_spec(VT_ALL), tile(D_GROUP), tile(D_GROUP),
                   tile(D_GROUP), tile(D_GROUP), vt_spec(SLOT)],
        out_shape=out_shape,
        scratch_shapes=[pltpu.VMEM((tm + CONV_HDR, D_GROUP), jnp.float32),
                        pltpu.VMEM((tm + POOL_HDR, D_GROUP), jnp.float32),
                        pltpu.VMEM((tm + POOL_HDR, D_GROUP), jnp.float32),
                        pltpu.VMEM((tm + POOL_HDR, D_GROUP), jnp.float32)],
        compiler_params=pltpu.CompilerParams(dimension_semantics=("arbitrary",),
                                             vmem_limit_bytes=40 * 1024 * 1024),
        name="in_stage",
    )(x2d, lw["g_attn"], lw["w_in"], lw["g_q"], lw["w_uq"], lw["g_kv"], lw["w_kv"],
      tabs["tq"], tabs["tk"], lw["conv_w"], lw["w_pool"], lw["pool_scale"], lw["mix_b"], lw["mix_c"])


def _mla_kernel(q_ref, k_ref, vt_ref, mix_ref, o_ref, acc_scr, s_scr):
    tq = q_ref.shape[0]
    i = pl.program_id(1)
    acc_scr[...] = jnp.zeros(acc_scr.shape, jnp.float32)

    def scores(h, cols, key_start):
        k_h = k_ref[pl.ds(key_start, TK), h * SLOT:(h + 1) * SLOT]
        return _dot_nt(k_h, q_ref[cols, h * SLOT:(h + 1) * SLOT])

    def accumulate(h, s, m_old, cols, key_start, mask):
        if mask is not None:
            s = jnp.where(mask, s, NEG_BIG)
        m_new = jnp.maximum(m_old, jnp.max(s, axis=0, keepdims=True))
        p = jnp.exp2(s - m_new).astype(jnp.bfloat16)
        vt_h = vt_ref[h * VT_ROWS:(h + 1) * VT_ROWS, pl.ds(key_start, TK)]
        acc_scr[h, :, cols] = jnp.exp2(m_old - m_new) * acc_scr[h, :, cols] + _dot(vt_h, p)
        return m_new

    def run_units(units, m, following):
        m = list(m)
        s_next = s_scr[...]
        for u, (h, cols, key_start, mask) in enumerate(units):
            s_cur = s_next
            nxt = units[u + 1] if u + 1 < len(units) else following
            if nxt is not None:
                s_next = scores(*nxt[:3])
            ncol = cols.stop - cols.start
            m_new = accumulate(h, s_cur, m[h][:, m[h].shape[1] - ncol:], cols, key_start, mask)
            m[h] = m_new
        if following is not None:
            s_scr[...] = s_next
        return tuple(m)

    all_cols = slice(0, tq)
    tiles_per_step = tq // TK
    tile_start = lambda t: pl.multiple_of(t * TK, TK)

    def full_tiles(j, m):
        units = []
        for t in range(tiles_per_step):
            units += [(h, all_cols, tile_start(j * tiles_per_step + t), None) for h in range(MLA_HEADS)]
        return run_units(units, m, following=(0, all_cols, tile_start((j + 1) * tiles_per_step)))

    s_scr[...] = scores(0, all_cols, 0)
    m = tuple(jnp.full((1, tq), NEG_BIG, jnp.float32) for _ in range(MLA_HEADS))
    m = lax.fori_loop(0, i, full_tiles, m)

    units = []
    for d in range(tiles_per_step):
        key_start = pl.multiple_of(i * tq + d * TK, TK)
        ncol = tq - d * TK
        mask = (lax.broadcasted_iota(jnp.int32, (TK, ncol), 0) <= lax.broadcasted_iota(jnp.int32, (TK, ncol), 1))
        units += [(h, slice(d * TK, tq), key_start, mask) for h in range(MLA_HEADS)]
    run_units(units, m, following=None)

    y_t = jnp.concatenate([acc_scr[h, 0:MLA_V, :] / acc_scr[h, MLA_V:MLA_V + 1, :] for h in range(MLA_HEADS)], axis=0)
    o_ref[...] = _rms(y_t.T, mix_ref[...]).astype(jnp.bfloat16)


def _mla_attention(q, k, vt, mix_a, batch, seq):
    nq = seq // TQ
    return pl.pallas_call(
        _mla_kernel,
        grid=(batch, nq),
        in_specs=[pl.BlockSpec((TQ, MLA_W), lambda b, i: (b * nq + i, 0)),
                  pl.BlockSpec((seq, MLA_W), lambda b, i: (b, 0)),
                  pl.BlockSpec((None, VT_ALL, seq), lambda b, i: (b, 0, 0)),
                  pl.BlockSpec((1, D_GROUP), lambda b, i: (0, 0))],
        out_specs=pl.BlockSpec((TQ, D_GROUP), lambda b, i: (b * nq + i, 0)),
        out_shape=jax.ShapeDtypeStruct((batch * seq, D_GROUP), jnp.bfloat16),
        scratch_shapes=[pltpu.VMEM((MLA_HEADS, VT_ROWS, TQ), jnp.float32),
                        pltpu.VMEM((TK, TQ), jnp.float32)],
        compiler_params=pltpu.CompilerParams(dimension_semantics=("arbitrary", "arbitrary"),
                                             vmem_limit_bytes=40 * 1024 * 1024),
        name="mla_attention",
    )(q, k, vt, mix_a)


def _swa_kernel(sinks_ref, q_ref, k_ref, vt_ref, bias_ref, mix_ref, o_ref, yt_scr):
    tq = q_ref.shape[0]
    blk = SWA_WINDOW
    i = pl.program_id(1)
    lane_q = lax.broadcasted_iota(jnp.int32, (blk, D_GROUP), 1)
    head_of_col = lax.broadcasted_iota(jnp.int32, (1, SWA_HEADS * blk), 1) // blk
    sink_row = jnp.zeros((1, SWA_HEADS * blk), jnp.float32)
    for h in range(SWA_HEADS):
        sink_row = jnp.where(head_of_col == h, sinks_ref[h] * LOG2E, sink_row)
    def key_start(jb):
        return pl.multiple_of(jnp.maximum(i * tq + (jb - 1) * blk, 0), blk)

    def scores(jb):
        k_t = k_ref[pl.ds(key_start(jb), 2 * blk), :]
        q_b = q_ref[jb * blk:(jb + 1) * blk, :]
        q_stack = jnp.concatenate(
            [jnp.where((lane_q >= h * SWA_HEAD_DIM) & (lane_q < (h + 1) * SWA_HEAD_DIM), q_b, 0)
             for h in range(SWA_HEADS)], axis=0)
        return _dot_nt(k_t, q_stack)

    n_blk = tq // blk
    s_next = scores(0)
    for jb in range(n_blk):
        s_cur = s_next
        if jb + 1 < n_blk:
            s_next = scores(jb + 1)
        vt_t = vt_ref[:, pl.ds(key_start(jb), 2 * blk)]
        s = s_cur + bias_ref[jnp.minimum(i * tq + jb * blk, 1)]
        m = jnp.maximum(jnp.max(s, axis=0, keepdims=True), sink_row)
        if jb + 1 < n_blk:
            m = m + _zero_after(s_next[0:1, :])
        p = jnp.exp2(s - m)
        inv = 1.0 / (jnp.sum(p, axis=0, keepdims=True) + jnp.exp2(sink_row - m))
        o = _dot(vt_t, p.astype(jnp.bfloat16)) * inv
        for h in range(SWA_HEADS):
            yt_scr[h * SWA_HEAD_DIM:(h + 1) * SWA_HEAD_DIM, jb * blk:(jb + 1) * blk] = (
                o[(h // 2) * SWA_HEAD_DIM:(h // 2 + 1) * SWA_HEAD_DIM, h * blk:(h + 1) * blk])
    o_ref[...] = _rms(yt_scr[...].T, mix_ref[...]).astype(jnp.bfloat16)


def _swa_bias_tables(slopes):
    blk = SWA_WINDOW
    key = np.arange(2 * blk)[:, None]
    qry = np.arange(blk)[None, :]
    tabs = []
    for off in (0, blk):
        dist = off + qry - key
        valid = (dist >= 0) & (dist < SWA_WINDOW)
        tabs.append(np.concatenate([np.where(valid, -s * LOG2E * dist, NEG_BIG) for s in slopes], axis=1))
    return jnp.asarray(np.stack(tabs), jnp.float32)


def _swa_attention(sinks, q, k, vt, bias, mix_d, batch, seq):
    tq = 512
    nq = seq // tq
    blk = SWA_WINDOW
    return pl.pallas_call(
        _swa_kernel,
        grid=(batch, nq),
        in_specs=[pl.BlockSpec(memory_space=pltpu.SMEM),
                  pl.BlockSpec((tq, D_GROUP), lambda b, i: (b * nq + i, 0)),
                  pl.BlockSpec((seq, D_GROUP), lambda b, i: (b, 0)),
                  pl.BlockSpec((None, SLOT, seq), lambda b, i: (b, 0, 0)),
                  pl.BlockSpec((2, 2 * blk, SWA_HEADS * blk), lambda b, i: (0, 0, 0)),
                  pl.BlockSpec((1, D_GROUP), lambda b, i: (0, 0))],
        out_specs=pl.BlockSpec((tq, D_GROUP), lambda b, i: (b * nq + i, 0)),
        out_shape=jax.ShapeDtypeStruct((batch * seq, D_GROUP), jnp.bfloat16),
        scratch_shapes=[pltpu.VMEM((D_GROUP, tq), jnp.float32)],
        compiler_params=pltpu.CompilerParams(dimension_semantics=("arbitrary", "arbitrary"),
                                             vmem_limit_bytes=40 * 1024 * 1024),
        name="swa_attention",
    )(sinks, q, k, vt, bias, mix_d)


def _out_stage_kernel(x_ref, ya_ref, yb_ref, yc_ref, yd_ref, w_o_ref, g_ffn_ref, w_gu_ref,
                      w_down_ref, g_final_ref, o_ref, act_scr, *, final):
    x = x_ref[...]
    for g, y_ref in enumerate((ya_ref, yb_ref, yc_ref, yd_ref)):
        x = x + _dot(y_ref[...], w_o_ref[g * D_GROUP:(g + 1) * D_GROUP, :])
    h2 = _rms(x, g_ffn_ref[...]).astype(jnp.bfloat16)

    for c in range(N_FF_CHUNKS):
        cols = slice(c * FF_CHUNK, (c + 1) * FF_CHUNK)
        gate = _dot(h2, w_gu_ref[:, cols])
        up = _dot(h2, w_gu_ref[:, D_FF + c * FF_CHUNK:D_FF + (c + 1) * FF_CHUNK])
        act_scr[:, cols] = (gate * jax.nn.sigmoid(gate) * up).astype(jnp.bfloat16)
    x = x + _dot(act_scr[...], w_down_ref[...])
    if final:
        x = _rms(x, g_final_ref[...])
    o_ref[...] = x


def _out_stage(x2d, ya, yb, yc, yd, lw, g_final, final):
    n_tok = x2d.shape[0]
    tm = TM_OUT
    const = lambda shape: pl.BlockSpec(shape, lambda t: (0,) * len(shape), pipeline_mode=pl.Buffered(1))
    tile = lambda w: pl.BlockSpec((tm, w), lambda t: (t, 0))
    return pl.pallas_call(
        functools.partial(_out_stage_kernel, final=final),
        grid=(n_tok // tm,),
        in_specs=[tile(D_MODEL), tile(D_GROUP), tile(D_GROUP), tile(D_GROUP), tile(D_GROUP),
                  const((D_MODEL, D_MODEL)), const((1, D_MODEL)), const((D_MODEL, 2 * D_FF)),
                  const((D_FF, D_MODEL)), const((1, D_MODEL))],
        out_specs=tile(D_MODEL),
        out_shape=jax.ShapeDtypeStruct((n_tok, D_MODEL), jnp.float32),
        scratch_shapes=[pltpu.VMEM((tm, D_FF), jnp.bfloat16)],
        compiler_params=pltpu.CompilerParams(dimension_semantics=("arbitrary",),
                                             vmem_limit_bytes=52 * 1024 * 1024),
        name="out_stage",
    )(x2d, ya, yb, yc, yd, lw["w_o"], lw["g_ffn"], lw["w_gate_up"], lw["w_down"], g_final)


def _rope_tables(seq):
    inv = 1.0 / (ROPE_THETA ** (jnp.arange(0, MLA_ROPE, 2, dtype=jnp.float32) / MLA_ROPE))
    ang = jnp.arange(seq, dtype=jnp.float32)[:, None] * inv[None, :]
    cos, sin = jnp.cos(ang), jnp.sin(ang)
    cos2 = jnp.concatenate([cos, cos], axis=1)
    sin2 = jnp.concatenate([sin, sin], axis=1)
    scale = LOG2E / math.sqrt(MLA_NOPE + MLA_ROPE)
    tq =jnp.concatenate([jnp.full((seq, MLA_NOPE), scale, jnp.float32), cos2 * scale, sin2 * scale], axis=1)
    tk = jnp.concatenate([jnp.zeros((seq, MLA_NOPE), jnp.float32), cos2, sin2], axis=1)
    return {"tq": tq, "tk": tk}


def _swap_halves(w):
    half = w.shape[-1] // 2
    return jnp.concatenate([-w[..., half:], w[..., :half]], axis=-1)


def _layer_weights(l, attn_norm, w_in, mla_q_norm, w_uq, mla_kv_norm, w_ukv, conv_w, pool_w, pool_scale,
                   mix_norm, w_o, ffn_norm, w_gate_up, w_down):
    bf = jnp.bfloat16
    f32 = jnp.float32
    wi = w_in[l]
    pts = np.cumsum((0, 256, 128, 32, 256, 256, 256, 256, 256, 128, 128))
    c_q, c_kv, k_r, g_b, g_c, u_conv, u_pool, q_sw, k_sw, v_sw = [wi[:, pts[j]:pts[j + 1]] for j in range(10)]
    zeros = lambda w: jnp.zeros((D_MODEL, w), f32)
    w_in_r = jnp.concatenate([c_q, c_kv, g_b, g_c, u_conv, u_pool, q_sw, k_sw, v_sw,
                              zeros(MLA_NOPE), k_r, _swap_halves(k_r)], axis=1)
    wq = w_uq[l].reshape(MLA_Q_RANK, MLA_HEADS, MLA_NOPE + MLA_ROPE)
    wq_rot = wq[..., MLA_NOPE:]
    w_uq_p = jnp.concatenate([wq, _swap_halves(wq_rot)], axis=-1).reshape(MLA_Q_RANK, MLA_W)
    wkv = w_ukv[l].reshape(MLA_KV_RANK, MLA_HEADS, MLA_NOPE + MLA_V)
    zk = jnp.zeros((MLA_KV_RANK, MLA_HEADS, SLOT - MLA_NOPE), f32)
    w_k = jnp.concatenate([wkv[..., :MLA_NOPE], zk], axis=-1).reshape(MLA_KV_RANK, MLA_W)
    zv = jnp.zeros((MLA_KV_RANK, MLA_HEADS, VT_ROWS - MLA_V), f32)
    w_v = jnp.concatenate([wkv[..., MLA_NOPE:], zv], axis=-1).reshape(MLA_KV_RANK, VT_ALL)
    w_v = jnp.concatenate([w_v, jnp.zeros((MLA_KV_RANK, VT_PAD - VT_ALL), f32)], axis=1)
    w_pool = jax.scipy.linalg.block_diag(*[pool_w[l, g] for g in range(len(POOL_WINDOWS))])
    mix = mix_norm[l].reshape(4, 1, D_GROUP)
    return {
        "g_attn": attn_norm[l][None, :], "w_in": w_in_r.astype(bf),
        "g_q": mla_q_norm[l][None, :], "w_uq": w_uq_p.astype(bf),
        "g_kv": mla_kv_norm[l][None, :], "w_kv": jnp.concatenate([w_k, w_v], axis=1).astype(bf),
        "conv_w": conv_w[l], "w_pool": w_pool.astype(bf), "pool_scale": pool_scale[l][None, :],
        "mix_a": mix[0], "mix_b": mix[1], "mix_c": mix[2], "mix_d": mix[3],
        "w_o": w_o[l].astype(bf), "g_ffn": ffn_norm[l][None, :],
        "w_gate_up": w_gate_up[l].astype(bf), "w_down": w_down[l].astype(bf),
    }


def kernel(x, attn_norm, w_in, mla_q_norm, w_uq, mla_kv_norm, w_ukv, conv_w, pool_w, pool_scale, swa_sinks,
           mix_norm, w_o, ffn_norm, w_gate_up, w_down, final_norm):
    batch, seq, d_model = x.shape
    depth = w_in.shape[0]
    assert d_model == D_MODEL and w_in.shape[2] == D_IN
    assert seq % TM_IN == 0 and seq % TQ == 0 and TQ % TK == 0 and (batch * seq) % TM_OUT == 0
    slopes = tuple(float(2.0 ** (-8.0 * (h + 1) / SWA_HEADS)) for h in range(SWA_HEADS))
    tabs = _rope_tables(seq)
    swa_bias = _swa_bias_tables(slopes)
    x2d = x.reshape(batch * seq, D_MODEL)
    g_final = final_norm[None, :]
    for l in range(depth):
        lw = _layer_weights(l, attn_norm, w_in, mla_q_norm, w_uq, mla_kv_norm, w_ukv, conv_w, pool_w,
                            pool_scale, mix_norm, w_o, ffn_norm, w_gate_up, w_down)
        q, k, vt, yb, yc, qsw, ksw, vswt = _in_stage(x2d, lw, tabs, seq)
        ya = _mla_attention(q, k, vt, lw["mix_a"], batch, seq)
        yd = _swa_attention(swa_sinks[l], qsw, ksw, vswt, swa_bias, lw["mix_d"], batch, seq)
        x2d = _out_stage(x2d, ya, yb, yc, yd, lw, g_final, final=(l == depth - 1))
    return x2d.reshape(batch, seq, D_MODEL)
```

```python
import functools
import math

import jax
import jax.numpy as jnp
import numpy as np
from jax import lax
from jax.experimental import pallas as pl
from jax.experimental.pallas import tpu as pltpu

D_MODEL = 1024
D_GROUP = 256
MLA_HEADS = 4
MLA_Q_RANK = 256
MLA_KV_RANK = 128
MLA_NOPE = 64
MLA_ROPE = 32
MLA_V = 64
ROPE_THETA = 10000.0
CONV_WIDTH = 3
POOL_WINDOWS = (2, 4, 8, 16)
POOL_CH = 64
SWA_HEADS = 4
SWA_KV_HEADS = 2
SWA_HEAD_DIM = 64
SWA_WINDOW = 128
D_FF = 2816
RMS_EPS = 1e-6
D_IN = 1952

LANES = 128
V7X_VMEM_BYTES = 64 * 1024 * 1024

D_IN_PAD = 2048
OFF_CQ, OFF_CKV, OFF_KR, OFF_GB, OFF_GC, OFF_UCONV, OFF_UPOOL = 0, 256, 384, 512, 768, 1024, 1280
OFF_QSW, OFF_KSW, OFF_VSW = 1536, 1792, 1920
IN_GROUPS = ((0, D_IN_PAD),)
SLOT = LANES
MLA_W = MLA_HEADS * SLOT
VT_ROWS = 80
VT_ALL = MLA_HEADS * VT_ROWS
VT_PAD = 384
LOG2E = math.log2(math.e)
POOL_HDR = 32
CONV_HDR = 8
NEG_BIG = -1e30

TM_IN = 1024
TQ = 512
TK = 256
TM_OUT = 512
FF_CHUNK = 256
N_FF_CHUNKS = D_FF // FF_CHUNK


def _rms(x, g):
    return x * lax.rsqrt(jnp.mean(x * x, axis=-1, keepdims=True) + RMS_EPS) * g


def _dot(a, b):
    return lax.dot_general(a, b, (((1,), (0,)), ((), ())), preferred_element_type=jnp.float32)


def _zero_after(x):
    bits = pltpu.bitcast(x, jnp.int32)
    return lax.shift_right_logical(lax.shift_right_logical(bits, 16), 16).astype(jnp.float32)


def _dot_nt(a, b):
    return lax.dot_general(a, b, (((1,), (1,)), ((), ())), preferred_element_type=jnp.float32)


def _in_stage_kernel(x_ref, g_attn_ref, w_in_ref, g_q_ref, w_uq_ref, g_kv_ref, w_kv_ref,
                     tq_ref, tk_ref, conv_w_ref, w_pool_ref, pool_scale_ref, mixb_ref, mixc_ref,
                     q_ref, k_ref, vt_ref, yb_ref, yc_ref, qsw_ref, ksw_ref, vswt_ref,
                     conv_scr, p0, p1, p2, *, tiles_per_seq):
    tm = x_ref.shape[0]
    t = pl.program_id(0)
    tile_in_seq = t % tiles_per_seq
    pos0 = pl.multiple_of(tile_in_seq * tm, tm)

    @pl.when(tile_in_seq == 0)
    def _():
        conv_scr[0:CONV_HDR, :] = jnp.zeros((CONV_HDR, D_GROUP), jnp.float32)
        p0[0:POOL_HDR, :] = jnp.zeros((POOL_HDR, D_GROUP), jnp.float32)

    h = _rms(x_ref[...], g_attn_ref[...]).astype(jnp.bfloat16)
    group_vals = {}

    def proj(off, width):
        lo, hi = next(g for g in IN_GROUPS if g[0] <= off < g[1])
        if lo not in group_vals:
            group_vals[lo] = _dot(h, w_in_ref[:, lo:hi])
        return group_vals[lo][:, off - lo:off - lo + width]

    lane = lax.broadcasted_iota(jnp.int32, (tm, MLA_W), 1) % SLOT
    qn = _rms(proj(OFF_CQ, MLA_Q_RANK), g_q_ref[...]).astype(jnp.bfloat16)
    qa = _dot(qn, w_uq_ref[...])
    tq_tab = tq_ref[pl.ds(pos0, tm), :]
    qp = qa * jnp.concatenate([tq_tab] * MLA_HEADS, axis=1)
    q_rot = pltpu.roll(qp, MLA_W - MLA_ROPE, axis=1)
    q = jnp.where(lane < MLA_NOPE + MLA_ROPE, qp, 0.0) + jnp.where(
        (lane >= MLA_NOPE) & (lane < MLA_NOPE + MLA_ROPE), q_rot, 0.0)
    q_ref[...] = q.astype(jnp.bfloat16)

    ckn = _rms(proj(OFF_CKV, MLA_KV_RANK), g_kv_ref[...]).astype(jnp.bfloat16)
    kv = _dot(ckn, w_kv_ref[...])
    kr = proj(OFF_KR, SLOT) * tk_ref[pl.ds(pos0, tm), :]
    lane1 = lax.broadcasted_iota(jnp.int32, (tm, SLOT), 1)
    kr = jnp.where((lane1 >= MLA_NOPE) & (lane1 < MLA_NOPE + MLA_ROPE),
                   kr + pltpu.roll(kr, SLOT - MLA_ROPE, axis=1), 0.0)
    k_ref[...] = (kv[:, :MLA_W] + jnp.concatenate([kr] * MLA_HEADS, axis=1)).astype(jnp.bfloat16)

    z = proj(OFF_GC, D_GROUP) * proj(OFF_UCONV, D_GROUP)
    conv_scr[CONV_HDR:CONV_HDR + tm, :] = z
    z1 = conv_scr[CONV_HDR - 1:CONV_HDR - 1 + tm, :]
    z2 = conv_scr[CONV_HDR - 2:CONV_HDR - 2 + tm, :]
    cw = conv_w_ref[...]
    y_b = proj(OFF_GB, D_GROUP) * (cw[0:1, :] * z2 + cw[1:2, :] * z1 + cw[2:3, :] * z)
    conv_scr[0:CONV_HDR, :] = z[tm - CONV_HDR:tm, :]
    yb_ref[...] = _rms(y_b, mixb_ref[...]).astype(jnp.bfloat16)

    u = proj(OFF_UPOOL, D_GROUP)
    n = tm + POOL_HDR
    p0[POOL_HDR:n, :] = u
    p1[8:n, :] = p0[8:n, :] + p0[7:n - 1, :]
    s2 = p1[POOL_HDR:n, :]
    p2[16:n, :] = p1[16:n, :] + p1[14:n - 2, :]
    s4 = p2[POOL_HDR:n, :]
    p1[24:n, :] = p2[24:n, :] + p2[20:n - 4, :]
    s8 = p1[POOL_HDR:n, :]
    s16 = s8 + p1[24:n - 8, :]
    p0[0:POOL_HDR, :] = u[tm - POOL_HDR:tm, :]
    lane_c = lax.broadcasted_iota(jnp.int32, (tm, D_GROUP), 1)
    row_c = lax.broadcasted_iota(jnp.int32, (tm, D_GROUP), 0)
    win = jnp.where(lane_c < POOL_CH, s2, jnp.where(lane_c < 2 * POOL_CH, s4,
                    jnp.where(lane_c < 3 * POOL_CH, s8, s16)))
    width = jnp.where(lane_c < POOL_CH, POOL_WINDOWS[0], jnp.where(lane_c < 2 * POOL_CH, POOL_WINDOWS[1],
                      jnp.where(lane_c < 3 * POOL_CH, POOL_WINDOWS[2], POOL_WINDOWS[3])))
    count = jnp.minimum(pos0 + row_c + 1, width).astype(jnp.float32)
    pooled = win / count - u
    y_c = _dot(pooled.astype(jnp.bfloat16), w_pool_ref[...]) * pool_scale_ref[...]
    yc_ref[...] = _rms(y_c, mixc_ref[...]).astype(jnp.bfloat16)

    qsw_ref[...] = (proj(OFF_QSW, D_GROUP) * (LOG2E / math.sqrt(SWA_HEAD_DIM))).astype(jnp.bfloat16)
    lane_s = lax.broadcasted_iota(jnp.int32, (tm, SLOT), 1)
    a = proj(OFF_KSW, SLOT)
    r = pltpu.roll(a, SWA_HEAD_DIM, axis=1)
    ksw_ref[...] = jnp.concatenate([jnp.where(lane_s < SWA_HEAD_DIM, a, r),
                                    jnp.where(lane_s < SWA_HEAD_DIM, r, a)], axis=1).astype(jnp.bfloat16)

    lane_v = lax.broadcasted_iota(jnp.int32, (tm, VT_PAD), 1)
    ones_col = jnp.where((lane_v % VT_ROWS == MLA_V) & (lane_v < VT_ALL), 1.0, 0.0)
    v_t = jnp.concatenate([proj(OFF_VSW, SLOT), kv[:, MLA_W:] + ones_col], axis=1).T
    vswt_ref[...] = v_t[:SLOT, :].astype(jnp.bfloat16)
    vt_ref[...] = v_t[SLOT:SLOT + VT_ALL, :].astype(jnp.bfloat16)


def _in_stage(x2d, lw, tabs, seq):
    n_tok = x2d.shape[0]
    tm = TM_IN
    tiles_per_seq = seq // tm
    const = lambda shape: pl.BlockSpec(shape, lambda t: (0, 0), pipeline_mode=pl.Buffered(1))
    tile = lambda w: pl.BlockSpec((tm, w), lambda t: (t, 0))
    vt_spec = lambda rows: pl.BlockSpec((None, rows, tm), lambda t: (t // tiles_per_seq, 0, t % tiles_per_seq))
    bf = jnp.bfloat16
    tok = lambda w: jax.ShapeDtypeStruct((n_tok, w), bf)
    out_shape = [tok(MLA_W), tok(MLA_W), jax.ShapeDtypeStruct((n_tok // seq, VT_ALL, seq), bf),
                 tok(D_GROUP), tok(D_GROUP), tok(D_GROUP), tok(D_GROUP),
                 jax.ShapeDtypeStruct((n_tok // seq, SLOT, seq), bf)]
    return pl.pallas_call(
        functools.partial(_in_stage_kernel, tiles_per_seq=tiles_per_seq),
        grid=(n_tok // tm,),
        in_specs=[tile(D_MODEL), const((1, D_MODEL)), const((D_MODEL, D_IN_PAD)),
                  const((1, MLA_Q_RANK)), const((MLA_Q_RANK, MLA_W)),
                  const((1, MLA_KV_RANK)), const((MLA_KV_RANK, MLA_W + VT_PAD)),
                  const((seq, SLOT)), const((seq, SLOT)),
                  const((CONV_WIDTH, D_GROUP)), const((D_GROUP, D_GROUP)), const((1, D_GROUP)),
                  const((1, D_GROUP)), const((1, D_GROUP))],
        out_specs=[tile(MLA_W), tile(MLA_W), vt_spec(VT_ALL), tile(D_GROUP), tile(D_GROUP),
                   tile(D_GROUP), tile(D_GROUP), vt_spec(SLOT)],
        out_shape=out_shape,
        scratch_shapes=[pltpu.VMEM((tm + CONV_HDR, D_GROUP), jnp.float32),
                        pltpu.VMEM((tm + POOL_HDR, D_GROUP), jnp.float32),
                        pltpu.VMEM((tm + POOL_HDR, D_GROUP), jnp.float32),
                        pltpu.VMEM((tm + POOL_HDR, D_GROUP), jnp.float32)],
        compiler_params=pltpu.CompilerParams(dimension_semantics=("arbitrary",),
                                             vmem_limit_bytes=40 * 1024 * 1024),
        name="in_stage",
    )(x2d, lw["g_attn"], lw["w_in"], lw["g_q"], lw["w_uq"], lw["g_kv"], lw["w_kv"],
      tabs["tq"], tabs["tk"], lw["conv_w"], lw["w_pool"], lw["pool_scale"], lw["mix_b"], lw["mix_c"])


def _mla_kernel(q_ref, k_ref, vt_ref, mix_ref, o_ref, acc_scr, s_scr):
    tq = q_ref.shape[0]
    i = pl.program_id(1)
    acc_scr[...] = jnp.zeros(acc_scr.shape, jnp.float32)

    def scores(h, cols, key_start):
        k_h = k_ref[pl.ds(key_start, TK), h * SLOT:(h + 1) * SLOT]
        return _dot_nt(k_h, q_ref[cols, h * SLOT:(h + 1) * SLOT])

    def accumulate(h, s, m_old, cols, key_start, mask):
        if mask is not None:
            s = jnp.where(mask, s, NEG_BIG)
        m_new = jnp.maximum(m_old, jnp.max(s, axis=0, keepdims=True))
        p = jnp.exp2(s - m_new).astype(jnp.bfloat16)
        vt_h = vt_ref[h * VT_ROWS:(h + 1) * VT_ROWS, pl.ds(key_start, TK)]
        acc_scr[h, :, cols] = jnp.exp2(m_old - m_new) * acc_scr[h, :, cols] + _dot(vt_h, p)
        return m_new

    def run_units(units, m, following):
        m = list(m)
        s_next = s_scr[...]
        for u, (h, cols, key_start, mask) in enumerate(units):
            s_cur = s_next
            nxt = units[u + 1] if u + 1 < len(units) else following
            if nxt is not None:
                s_next = scores(*nxt[:3])
            ncol = cols.stop - cols.start
            m_new = accumulate(h, s_cur, m[h][:, m[h].shape[1] - ncol:], cols, key_start, mask)
            m[h] = m_new
        if following is not None:
            s_scr[...] = s_next
        return tuple(m)

    all_cols = slice(0, tq)
    tiles_per_step = tq // TK
    tile_start = lambda t: pl.multiple_of(t * TK, TK)

    def full_tiles(j, m):
        units = []
        for t in range(tiles_per_step):
            units += [(h, all_cols, tile_start(j * tiles_per_step + t), None) for h in range(MLA_HEADS)]
        return run_units(units, m, following=(0, all_cols, tile_start((j + 1) * tiles_per_step)))

    s_scr[...] = scores(0, all_cols, 0)
    m = tuple(jnp.full((1, tq), NEG_BIG, jnp.float32) for _ in range(MLA_HEADS))
    m = lax.fori_loop(0, i, full_tiles, m)

    units = []
    for d in range(tiles_per_step):
        key_start = pl.multiple_of(i * tq + d * TK, TK)
        ncol = tq - d * TK
        mask = (lax.broadcasted_iota(jnp.int32, (TK, ncol), 0) <= lax.broadcasted_iota(jnp.int32, (TK, ncol), 1))
        units += [(h, slice(d * TK, tq), key_start, mask) for h in range(MLA_HEADS)]
    run_units(units, m, following=None)

    y_t = jnp.concatenate([acc_scr[h, 0:MLA_V, :] / acc_scr[h, MLA_V:MLA_V + 1, :] for h in range(MLA_HEADS)], axis=0)
    o_ref[...] = _rms(y_t.T, mix_ref[...]).astype(jnp.bfloat16)


def _mla_attention(q, k, vt, mix_a, batch, seq):
    nq = seq // TQ
    return pl.pallas_call(
        _mla_kernel,
        grid=(batch, nq),
        in_specs=[pl.BlockSpec((TQ, MLA_W), lambda b, i: (b * nq + i, 0)),
                  pl.BlockSpec((seq, MLA_W), lambda b, i: (b, 0)),
                  pl.BlockSpec((None, VT_ALL, seq), lambda b, i: (b, 0, 0)),
                  pl.BlockSpec((1, D_GROUP), lambda b, i: (0, 0))],
        out_specs=pl.BlockSpec((TQ, D_GROUP), lambda b, i: (b * nq + i, 0)),
        out_shape=jax.ShapeDtypeStruct((batch * seq, D_GROUP), jnp.bfloat16),
        scratch_shapes=[pltpu.VMEM((MLA_HEADS, VT_ROWS, TQ), jnp.float32),
                        pltpu.VMEM((TK, TQ), jnp.float32)],
        compiler_params=pltpu.CompilerParams(dimension_semantics=("arbitrary", "arbitrary"),
                                             vmem_limit_bytes=40 * 1024 * 1024),
        name="mla_attention",
    )(q, k, vt, mix_a)


def _swa_kernel(sinks_ref, q_ref, k_ref, vt_ref, bias_ref, mix_ref, o_ref, yt_scr):
    tq = q_ref.shape[0]
    blk = SWA_WINDOW
    i = pl.program_id(1)
    lane_q = lax.broadcasted_iota(jnp.int32, (blk, D_GROUP), 1)
    head_of_col = lax.broadcasted_iota(jnp.int32, (1, SWA_HEADS * blk), 1) // blk
    sink_row = jnp.zeros((1, SWA_HEADS * blk), jnp.float32)
    for h in range(SWA_HEADS):
        sink_row = jnp.where(head_of_col == h, sinks_ref[h] * LOG2E, sink_row)
    def key_start(jb):
        return pl.multiple_of(jnp.maximum(i * tq + (jb - 1) * blk, 0), blk)

    def scores(jb):
        k_t = k_ref[pl.ds(key_start(jb), 2 * blk), :]
        q_b = q_ref[jb * blk:(jb + 1) * blk, :]
        q_stack = jnp.concatenate(
            [jnp.where((lane_q >= h * SWA_HEAD_DIM) & (lane_q < (h + 1) * SWA_HEAD_DIM), q_b, 0)
             for h in range(SWA_HEADS)], axis=0)
        return _dot_nt(k_t, q_stack)

    n_blk = tq // blk
    s_next = scores(0)
    for jb in range(n_blk):
        s_cur = s_next
        if jb + 1 < n_blk:
            s_next = scores(jb + 1)
        vt_t = vt_ref[:, pl.ds(key_start(jb), 2 * blk)]
        s = s_cur + bias_ref[jnp.minimum(i * tq + jb * blk, 1)]
        m = jnp.maximum(jnp.max(s, axis=0, keepdims=True), sink_row)
        if jb + 1 < n_blk:
            m = m + _zero_after(s_next[0:1, :])
        p = jnp.exp2(s - m)
        inv = 1.0 / (jnp.sum(p, axis=0, keepdims=True) + jnp.exp2(sink_row - m))
        o = _dot(vt_t, p.astype(jnp.bfloat16)) * inv
        for h in range(SWA_HEADS):
            yt_scr[h * SWA_HEAD_DIM:(h + 1) * SWA_HEAD_DIM, jb * blk:(jb + 1) * blk] = (
                o[(h // 2) * SWA_HEAD_DIM:(h // 2 + 1) * SWA_HEAD_DIM, h * blk:(h + 1) * blk])
    o_ref[...] = _rms(yt_scr[...].T, mix_ref[...]).astype(jnp.bfloat16)


def _swa_bias_tables(slopes):
    blk = SWA_WINDOW
    key = np.arange(2 * blk)[:, None]
    qry = np.arange(blk)[None, :]
    tabs = []
    for off in (0, blk):
        dist = off + qry - key
        valid = (dist >= 0) & (dist < SWA_WINDOW)
        tabs.append(np.concatenate([np.where(valid, -s * LOG2E * dist, NEG_BIG) for s in slopes], axis=1))
    return jnp.asarray(np.stack(tabs), jnp.float32)


def _swa_attention(sinks, q, k, vt, bias, mix_d, batch, seq):
    tq = 1024
    nq = seq // tq
    blk = SWA_WINDOW
    return pl.pallas_call(
        _swa_kernel,
        grid=(batch, nq),
        in_specs=[pl.BlockSpec(memory_space=pltpu.SMEM),
                  pl.BlockSpec((tq, D_GROUP), lambda b, i: (b * nq + i, 0)),
                  pl.BlockSpec((seq, D_GROUP), lambda b, i: (b, 0)),
                  pl.BlockSpec((None, SLOT, seq), lambda b, i: (b, 0, 0)),
                  pl.BlockSpec((2, 2 * blk, SWA_HEADS * blk), lambda b, i: (0, 0, 0)),
                  pl.BlockSpec((1, D_GROUP), lambda b, i: (0, 0))],
        out_specs=pl.BlockSpec((tq, D_GROUP), lambda b, i: (b * nq + i, 0)),
        out_shape=jax.ShapeDtypeStruct((batch * seq, D_GROUP), jnp.bfloat16),
        scratch_shapes=[pltpu.VMEM((D_GROUP, tq), jnp.float32)],
        compiler_params=pltpu.CompilerParams(dimension_semantics=("arbitrary", "arbitrary"),
                                             vmem_limit_bytes=40 * 1024 * 1024),
        name="swa_attention",
    )(sinks, q, k, vt, bias, mix_d)


def _out_stage_kernel(x_ref, ya_ref, yb_ref, yc_ref, yd_ref, w_o_ref, g_ffn_ref, w_gu_ref,
                      w_down_ref, g_final_ref, o_ref, act_scr, *, final):
    x = x_ref[...]
    for g, y_ref in enumerate((ya_ref, yb_ref, yc_ref, yd_ref)):
        x = x + _dot(y_ref[...], w_o_ref[g * D_GROUP:(g + 1) * D_GROUP, :])
    h2 = _rms(x, g_ffn_ref[...]).astype(jnp.bfloat16)

    for c in range(N_FF_CHUNKS):
        cols = slice(c * FF_CHUNK, (c + 1) * FF_CHUNK)
        gate = _dot(h2, w_gu_ref[:, cols])
        up = _dot(h2, w_gu_ref[:, D_FF + c * FF_CHUNK:D_FF + (c + 1) * FF_CHUNK])
        act_scr[:, cols] = (gate * jax.nn.sigmoid(gate) * up).astype(jnp.bfloat16)
    x = x + _dot(act_scr[...], w_down_ref[...])
    if final:
        x = _rms(x, g_final_ref[...])
    o_ref[...] = x


def _out_stage(x2d, ya, yb, yc, yd, lw, stacked, l, g_final, final):
    n_tok = x2d.shape[0]
    tm = TM_OUT
    const = lambda shape: pl.BlockSpec(shape, lambda t: (0,) * len(shape), pipeline_mode=pl.Buffered(1))
    of_layer = lambda shape: pl.BlockSpec((None,) + shape, lambda t: (l,) + (0,) * len(shape),
                                          pipeline_mode=pl.Buffered(1))
    tile = lambda w: pl.BlockSpec((tm, w), lambda t: (t, 0))
    return pl.pallas_call(
        functools.partial(_out_stage_kernel, final=final),
        grid=(n_tok // tm,),
        in_specs=[tile(D_MODEL), tile(D_GROUP), tile(D_GROUP), tile(D_GROUP), tile(D_GROUP),
                  of_layer((D_MODEL, D_MODEL)), const((1, D_MODEL)), of_layer((D_MODEL, 2 * D_FF)),
                  of_layer((D_FF, D_MODEL)), const((1, D_MODEL))],
        out_specs=tile(D_MODEL),
        out_shape=jax.ShapeDtypeStruct((n_tok, D_MODEL), jnp.float32),
        scratch_shapes=[pltpu.VMEM((tm, D_FF), jnp.bfloat16)],
        compiler_params=pltpu.CompilerParams(dimension_semantics=("arbitrary",),
                                             vmem_limit_bytes=52 * 1024 * 1024),
        name="out_stage",
    )(x2d, ya, yb, yc, yd, stacked["w_o"], lw["g_ffn"], stacked["w_gate_up"], stacked["w_down"], g_final)


def _rope_tables(seq):
    inv = 1.0 / (ROPE_THETA ** (jnp.arange(0, MLA_ROPE, 2, dtype=jnp.float32) / MLA_ROPE))
    ang = jnp.arange(seq, dtype=jnp.float32)[:, None] * inv[None, :]
    cos, sin = jnp.cos(ang), jnp.sin(ang)
    cos2 = jnp.concatenate([cos, cos], axis=1)
    sin2 = jnp.concatenate([sin, sin], axis=1)
    scale = LOG2E / math.sqrt(MLA_NOPE + MLA_ROPE)
    tq =jnp.concatenate([jnp.full((seq, MLA_NOPE), scale, jnp.float32), cos2 * scale, sin2 * scale], axis=1)
    tk = jnp.concatenate([jnp.zeros((seq, MLA_NOPE), jnp.float32), cos2, sin2], axis=1)
    return {"tq": tq, "tk": tk}


def _swap_halves(w):
    half = w.shape[-1] // 2
    return jnp.concatenate([-w[..., half:], w[..., :half]], axis=-1)


def _layer_weights(l, attn_norm, w_in, mla_q_norm, w_uq, mla_kv_norm, w_ukv, conv_w, pool_w, pool_scale,
                   mix_norm, ffn_norm):
    bf = jnp.bfloat16
    f32 = jnp.float32
    wi = w_in[l]
    pts = np.cumsum((0, 256, 128, 32, 256, 256, 256, 256, 256, 128, 128))
    c_q, c_kv, k_r, g_b, g_c, u_conv, u_pool, q_sw, k_sw, v_sw = [wi[:, pts[j]:pts[j + 1]] for j in range(10)]
    zeros = lambda w: jnp.zeros((D_MODEL, w), f32)
    w_in_r = jnp.concatenate([c_q, c_kv, zeros(MLA_NOPE), k_r, _swap_halves(k_r),
                              g_b, g_c, u_conv, u_pool, q_sw, k_sw, v_sw], axis=1)
    wq = w_uq[l].reshape(MLA_Q_RANK, MLA_HEADS, MLA_NOPE + MLA_ROPE)
    wq_rot = wq[..., MLA_NOPE:]
    w_uq_p = jnp.concatenate([wq, _swap_halves(wq_rot)], axis=-1).reshape(MLA_Q_RANK, MLA_W)
    wkv = w_ukv[l].reshape(MLA_KV_RANK, MLA_HEADS, MLA_NOPE + MLA_V)
    zk = jnp.zeros((MLA_KV_RANK, MLA_HEADS, SLOT - MLA_NOPE), f32)
    w_k = jnp.concatenate([wkv[..., :MLA_NOPE], zk], axis=-1).reshape(MLA_KV_RANK, MLA_W)
    zv = jnp.zeros((MLA_KV_RANK, MLA_HEADS, VT_ROWS - MLA_V), f32)
    w_v = jnp.concatenate([wkv[..., MLA_NOPE:], zv], axis=-1).reshape(MLA_KV_RANK, VT_ALL)
    w_v = jnp.concatenate([w_v, jnp.zeros((MLA_KV_RANK, VT_PAD - VT_ALL), f32)], axis=1)
    w_pool = jax.scipy.linalg.block_diag(*[pool_w[l, g] for g in range(len(POOL_WINDOWS))])
    mix = mix_norm[l].reshape(4, 1, D_GROUP)
    return {
        "g_attn": attn_norm[l][None, :], "w_in": w_in_r.astype(bf),
        "g_q": mla_q_norm[l][None, :], "w_uq": w_uq_p.astype(bf),
        "g_kv": mla_kv_norm[l][None, :], "w_kv": jnp.concatenate([w_k, w_v], axis=1).astype(bf),
        "conv_w": conv_w[l], "w_pool": w_pool.astype(bf), "pool_scale": pool_scale[l][None, :],
        "mix_a": mix[0], "mix_b": mix[1], "mix_c": mix[2], "mix_d": mix[3],
        "g_ffn": ffn_norm[l][None, :],
    }


def kernel(x, attn_norm, w_in, mla_q_norm, w_uq, mla_kv_norm, w_ukv, conv_w, pool_w, pool_scale, swa_sinks,
           mix_norm, w_o, ffn_norm, w_gate_up, w_down, final_norm):
    batch, seq, d_model = x.shape
    depth = w_in.shape[0]
    assert d_model == D_MODEL and w_in.shape[2] == D_IN
    assert seq % TM_IN == 0 and seq % TQ == 0 and TQ % TK == 0 and (batch * seq) % TM_OUT == 0
    slopes = tuple(float(2.0 ** (-8.0 * (h + 1) / SWA_HEADS)) for h in range(SWA_HEADS))
    tabs = _rope_tables(seq)
    swa_bias = _swa_bias_tables(slopes)
    x2d = x.reshape(batch * seq, D_MODEL)
    g_final = final_norm[None, :]
    stacked = {"w_o": w_o.astype(jnp.bfloat16), "w_gate_up": w_gate_up.astype(jnp.bfloat16),
               "w_down": w_down.astype(jnp.bfloat16)}
    for l in range(depth):
        lw = _layer_weights(l, attn_norm, w_in, mla_q_norm, w_uq, mla_kv_norm, w_ukv, conv_w, pool_w,
                            pool_scale, mix_norm, ffn_norm)
        q, k, vt, yb, yc, qsw, ksw, vswt = _in_stage(x2d, lw, tabs, seq)
        ya = _mla_attention(q, k, vt, lw["mix_a"], batch, seq)
        yd = _swa_attention(swa_sinks[l], qsw, ksw, vswt, swa_bias, lw["mix_d"], batch, seq)
        x2d = _out_stage(x2d, ya, yb, yc, yd, lw, stacked, l, g_final, final=(l == depth - 1))
    return x2d.reshape(batch, seq, D_MODEL)
```

```python
import functools
import math

import jax
import jax.numpy as jnp
import numpy as np
from jax import lax
from jax.experimental import pallas as pl
from jax.experimental.pallas import tpu as pltpu

D_MODEL = 1024
D_GROUP = 256
MLA_HEADS = 4
MLA_Q_RANK = 256
MLA_KV_RANK = 128
MLA_NOPE = 64
MLA_ROPE = 32
MLA_V = 64
ROPE_THETA = 10000.0
CONV_WIDTH = 3
POOL_WINDOWS = (2, 4, 8, 16)
POOL_CH = 64
SWA_HEADS = 4
SWA_KV_HEADS = 2
SWA_HEAD_DIM = 64
SWA_WINDOW = 128
D_FF = 2816
RMS_EPS = 1e-6
D_IN = 1952

LANES = 128
V7X_VMEM_BYTES = 64 * 1024 * 1024

D_IN_PAD = 2048
OFF_CQ, OFF_CKV, OFF_KR, OFF_GB, OFF_GC, OFF_UCONV, OFF_UPOOL = 0, 256, 384, 512, 768, 1024, 1280
OFF_QSW, OFF_KSW, OFF_VSW = 1536, 1792, 1920
IN_GROUPS = ((0, D_IN_PAD),)
SLOT = LANES
MLA_W = MLA_HEADS * SLOT
VT_ROWS = 80
VT_ALL = MLA_HEADS * VT_ROWS
VT_PAD = 384
LOG2E = math.log2(math.e)
POOL_HDR = 32
CONV_HDR = 8
NEG_BIG = -1e30

TM_IN = 1024
TQ = 512
TK = 512
TM_OUT = 1024
FF_CHUNK = 256
N_FF_CHUNKS = D_FF // FF_CHUNK


def _rms(x, g):
    return x * lax.rsqrt(jnp.mean(x * x, axis=-1, keepdims=True) + RMS_EPS) * g


def _dot(a, b):
    return lax.dot_general(a, b, (((1,), (0,)), ((), ())), preferred_element_type=jnp.float32)


def _zero_after(x):
    bits = pltpu.bitcast(x, jnp.int32)
    return lax.shift_right_logical(lax.shift_right_logical(bits, 16), 16).astype(jnp.float32)


def _dot_nt(a, b):
    return lax.dot_general(a, b, (((1,), (1,)), ((), ())), preferred_element_type=jnp.float32)


def _in_stage_kernel(x_ref, g_attn_ref, w_in_ref, g_q_ref, w_uq_ref, g_kv_ref, w_kv_ref,
                     tq_ref, tk_ref, conv_w_ref, w_pool_ref, pool_scale_ref, mixb_ref, mixc_ref,
                     q_ref, k_ref, vt_ref, yb_ref, yc_ref, qsw_ref, ksw_ref, vswt_ref,
                     conv_scr, p0, p1, p2, *, tiles_per_seq):
    tm = x_ref.shape[0]
    t = pl.program_id(0)
    tile_in_seq = t % tiles_per_seq
    pos0 = pl.multiple_of(tile_in_seq * tm, tm)

    @pl.when(tile_in_seq == 0)
    def _():
        conv_scr[0:CONV_HDR, :] = jnp.zeros((CONV_HDR, D_GROUP), jnp.float32)
        p0[0:POOL_HDR, :] = jnp.zeros((POOL_HDR, D_GROUP), jnp.float32)

    h = _rms(x_ref[...], g_attn_ref[...]).astype(jnp.bfloat16)
    group_vals = {}

    def proj(off, width):
        lo, hi = next(g for g in IN_GROUPS if g[0] <= off < g[1])
        if lo not in group_vals:
            group_vals[lo] = _dot(h, w_in_ref[:, lo:hi])
        return group_vals[lo][:, off - lo:off - lo + width]

    lane = lax.broadcasted_iota(jnp.int32, (tm, MLA_W), 1) % SLOT
    qn = _rms(proj(OFF_CQ, MLA_Q_RANK), g_q_ref[...]).astype(jnp.bfloat16)
    qa = _dot(qn, w_uq_ref[...])
    tq_tab = tq_ref[pl.ds(pos0, tm), :]
    qp = qa * jnp.concatenate([tq_tab] * MLA_HEADS, axis=1)
    q_rot = pltpu.roll(qp, MLA_W - MLA_ROPE, axis=1)
    q = jnp.where(lane < MLA_NOPE + MLA_ROPE, qp, 0.0) + jnp.where(
        (lane >= MLA_NOPE) & (lane < MLA_NOPE + MLA_ROPE), q_rot, 0.0)
    q_ref[...] = q.astype(jnp.bfloat16)

    ckn = _rms(proj(OFF_CKV, MLA_KV_RANK), g_kv_ref[...]).astype(jnp.bfloat16)
    kv = _dot(ckn, w_kv_ref[...])
    kr = proj(OFF_KR, SLOT) * tk_ref[pl.ds(pos0, tm), :]
    lane1 = lax.broadcasted_iota(jnp.int32, (tm, SLOT), 1)
    kr = jnp.where((lane1 >= MLA_NOPE) & (lane1 < MLA_NOPE + MLA_ROPE),
                   kr + pltpu.roll(kr, SLOT - MLA_ROPE, axis=1), 0.0)
    k_ref[...] = (kv[:, :MLA_W] + jnp.concatenate([kr] * MLA_HEADS, axis=1)).astype(jnp.bfloat16)

    z = proj(OFF_GC, D_GROUP) * proj(OFF_UCONV, D_GROUP)
    conv_scr[CONV_HDR:CONV_HDR + tm, :] = z
    z1 = conv_scr[CONV_HDR - 1:CONV_HDR - 1 + tm, :]
    z2 = conv_scr[CONV_HDR - 2:CONV_HDR - 2 + tm, :]
    cw = conv_w_ref[...]
    y_b = proj(OFF_GB, D_GROUP) * (cw[0:1, :] * z2 + cw[1:2, :] * z1 + cw[2:3, :] * z)
    conv_scr[0:CONV_HDR, :] = z[tm - CONV_HDR:tm, :]
    yb_ref[...] = _rms(y_b, mixb_ref[...]).astype(jnp.bfloat16)

    u = proj(OFF_UPOOL, D_GROUP)
    n = tm + POOL_HDR
    p0[POOL_HDR:n, :] = u
    p1[8:n, :] = p0[8:n, :] + p0[7:n - 1, :]
    s2 = p1[POOL_HDR:n, :]
    p2[16:n, :] = p1[16:n, :] + p1[14:n - 2, :]
    s4 = p2[POOL_HDR:n, :]
    p1[24:n, :] = p2[24:n, :] + p2[20:n - 4, :]
    s8 = p1[POOL_HDR:n, :]
    s16 = s8 + p1[24:n - 8, :]
    p0[0:POOL_HDR, :] = u[tm - POOL_HDR:tm, :]
    lane_c = lax.broadcasted_iota(jnp.int32, (tm, D_GROUP), 1)
    row_c = lax.broadcasted_iota(jnp.int32, (tm, D_GROUP), 0)
    win = jnp.where(lane_c < POOL_CH, s2, jnp.where(lane_c < 2 * POOL_CH, s4,
                    jnp.where(lane_c < 3 * POOL_CH, s8, s16)))
    width = jnp.where(lane_c < POOL_CH, POOL_WINDOWS[0], jnp.where(lane_c < 2 * POOL_CH, POOL_WINDOWS[1],
                      jnp.where(lane_c < 3 * POOL_CH, POOL_WINDOWS[2], POOL_WINDOWS[3])))
    count = jnp.minimum(pos0 + row_c + 1, width).astype(jnp.float32)
    pooled = win / count - u
    y_c = _dot(pooled.astype(jnp.bfloat16), w_pool_ref[...]) * pool_scale_ref[...]
    yc_ref[...] = _rms(y_c, mixc_ref[...]).astype(jnp.bfloat16)

    qsw_ref[...] = (proj(OFF_QSW, D_GROUP) * (LOG2E / math.sqrt(SWA_HEAD_DIM))).astype(jnp.bfloat16)
    lane_s = lax.broadcasted_iota(jnp.int32, (tm, SLOT), 1)
    a = proj(OFF_KSW, SLOT)
    r = pltpu.roll(a, SWA_HEAD_DIM, axis=1)
    ksw_ref[...] = jnp.concatenate([jnp.where(lane_s < SWA_HEAD_DIM, a, r),
                                    jnp.where(lane_s < SWA_HEAD_DIM, r, a)], axis=1).astype(jnp.bfloat16)

    lane_v = lax.broadcasted_iota(jnp.int32, (tm, VT_PAD), 1)
    ones_col = jnp.where((lane_v % VT_ROWS == MLA_V) & (lane_v < VT_ALL), 1.0, 0.0)
    v_t = jnp.concatenate([proj(OFF_VSW, SLOT), kv[:, MLA_W:] + ones_col], axis=1).T
    vswt_ref[...] = v_t[:SLOT, :].astype(jnp.bfloat16)
    vt_ref[...] = v_t[SLOT:SLOT + VT_ALL, :].astype(jnp.bfloat16)


def _in_stage(x2d, lw, tabs, seq):
    n_tok = x2d.shape[0]
    tm = TM_IN
    tiles_per_seq = seq // tm
    const = lambda shape: pl.BlockSpec(shape, lambda t: (0, 0), pipeline_mode=pl.Buffered(1))
    tile = lambda w: pl.BlockSpec((tm, w), lambda t: (t, 0))
    vt_spec = lambda rows: pl.BlockSpec((None, rows, tm), lambda t: (t // tiles_per_seq, 0, t % tiles_per_seq))
    bf = jnp.bfloat16
    tok = lambda w: jax.ShapeDtypeStruct((n_tok, w), bf)
    out_shape = [tok(MLA_W), tok(MLA_W), jax.ShapeDtypeStruct((n_tok // seq, VT_ALL, seq), bf),
                 tok(D_GROUP), tok(D_GROUP), tok(D_GROUP), tok(D_GROUP),
                 jax.ShapeDtypeStruct((n_tok // seq, SLOT, seq), bf)]
    return pl.pallas_call(
        functools.partial(_in_stage_kernel, tiles_per_seq=tiles_per_seq),
        grid=(n_tok // tm,),
        in_specs=[tile(D_MODEL), const((1, D_MODEL)), const((D_MODEL, D_IN_PAD)),
                  const((1, MLA_Q_RANK)), const((MLA_Q_RANK, MLA_W)),
                  const((1, MLA_KV_RANK)), const((MLA_KV_RANK, MLA_W + VT_PAD)),
                  const((seq, SLOT)), const((seq, SLOT)),
                  const((CONV_WIDTH, D_GROUP)), const((D_GROUP, D_GROUP)), const((1, D_GROUP)),
                  const((1, D_GROUP)), const((1, D_GROUP))],
        out_specs=[tile(MLA_W), tile(MLA_W), vt_spec(VT_ALL), tile(D_GROUP), tile(D_GROUP),
                   tile(D_GROUP), tile(D_GROUP), vt_spec(SLOT)],
        out_shape=out_shape,
        scratch_shapes=[pltpu.VMEM((tm + CONV_HDR, D_GROUP), jnp.float32),
                        pltpu.VMEM((tm + POOL_HDR, D_GROUP), jnp.float32),
                        pltpu.VMEM((tm + POOL_HDR, D_GROUP), jnp.float32),
                        pltpu.VMEM((tm + POOL_HDR, D_GROUP), jnp.float32)],
        compiler_params=pltpu.CompilerParams(dimension_semantics=("arbitrary",),
                                             vmem_limit_bytes=40 * 1024 * 1024),
        name="in_stage",
    )(x2d, lw["g_attn"], lw["w_in"], lw["g_q"], lw["w_uq"], lw["g_kv"], lw["w_kv"],
      tabs["tq"], tabs["tk"], lw["conv_w"], lw["w_pool"], lw["pool_scale"], lw["mix_b"], lw["mix_c"])


def _mla_kernel(q_ref, k_ref, vt_ref, mix_ref, o_ref, acc_scr, s_scr):
    tq = q_ref.shape[0]
    i = pl.program_id(1)
    acc_scr[...] = jnp.zeros(acc_scr.shape, jnp.float32)

    def scores(h, cols, key_start):
        k_h = k_ref[pl.ds(key_start, TK), h * SLOT:(h + 1) * SLOT]
        return _dot_nt(k_h, q_ref[cols, h * SLOT:(h + 1) * SLOT])

    def accumulate(h, s, m_old, cols, key_start, mask):
        if mask is not None:
            s = jnp.where(mask, s, NEG_BIG)
        m_new = jnp.maximum(m_old, jnp.max(s, axis=0, keepdims=True))
        p = jnp.exp2(s - m_new).astype(jnp.bfloat16)
        vt_h = vt_ref[h * VT_ROWS:(h + 1) * VT_ROWS, pl.ds(key_start, TK)]
        acc_scr[h, :, cols] = jnp.exp2(m_old - m_new) * acc_scr[h, :, cols] + _dot(vt_h, p)
        return m_new

    def run_units(units, m, following):
        m = list(m)
        s_next = s_scr[...]
        for u, (h, cols, key_start, mask) in enumerate(units):
            s_cur = s_next
            nxt = units[u + 1] if u + 1 < len(units) else following
            if nxt is not None:
                s_next = scores(*nxt[:3])
            ncol = cols.stop - cols.start
            m_new = accumulate(h, s_cur, m[h][:, m[h].shape[1] - ncol:], cols, key_start, mask)
            m[h] = m_new
        if following is not None:
            s_scr[...] = s_next
        return tuple(m)

    all_cols = slice(0, tq)
    tiles_per_step = tq // TK
    tile_start = lambda t: pl.multiple_of(t * TK, TK)

    def full_tiles(j, m):
        units = []
        for t in range(tiles_per_step):
            units += [(h, all_cols, tile_start(j * tiles_per_step + t), None) for h in range(MLA_HEADS)]
        return run_units(units, m, following=(0, all_cols, tile_start((j + 1) * tiles_per_step)))

    s_scr[...] = scores(0, all_cols, 0)
    m = tuple(jnp.full((1, tq), NEG_BIG, jnp.float32) for _ in range(MLA_HEADS))
    m = lax.fori_loop(0, i, full_tiles, m)

    units = []
    for d in range(tiles_per_step):
        key_start = pl.multiple_of(i * tq + d * TK, TK)
        ncol = tq - d * TK
        mask = (lax.broadcasted_iota(jnp.int32, (TK, ncol), 0) <= lax.broadcasted_iota(jnp.int32, (TK, ncol), 1))
        units += [(h, slice(d * TK, tq), key_start, mask) for h in range(MLA_HEADS)]
    run_units(units, m, following=None)

    y_t = jnp.concatenate([acc_scr[h, 0:MLA_V, :] / acc_scr[h, MLA_V:MLA_V + 1, :] for h in range(MLA_HEADS)], axis=0)
    o_ref[...] = _rms(y_t.T, mix_ref[...]).astype(jnp.bfloat16)


def _mla_attention(q, k, vt, mix_a, batch, seq):
    nq = seq // TQ
    return pl.pallas_call(
        _mla_kernel,
        grid=(batch, nq),
        in_specs=[pl.BlockSpec((TQ, MLA_W), lambda b, i: (b * nq + i, 0)),
                  pl.BlockSpec((seq, MLA_W), lambda b, i: (b, 0)),
                  pl.BlockSpec((None, VT_ALL, seq), lambda b, i: (b, 0, 0)),
                  pl.BlockSpec((1, D_GROUP), lambda b, i: (0, 0))],
        out_specs=pl.BlockSpec((TQ, D_GROUP), lambda b, i: (b * nq + i, 0)),
        out_shape=jax.ShapeDtypeStruct((batch * seq, D_GROUP), jnp.bfloat16),
        scratch_shapes=[pltpu.VMEM((MLA_HEADS, VT_ROWS, TQ), jnp.float32),
                        pltpu.VMEM((TK, TQ), jnp.float32)],
        compiler_params=pltpu.CompilerParams(dimension_semantics=("arbitrary", "arbitrary"),
                                             vmem_limit_bytes=40 * 1024 * 1024),
        name="mla_attention",
    )(q, k, vt, mix_a)


def _swa_kernel(sinks_ref, q_ref, k_ref, vt_ref, bias_ref, mix_ref, o_ref, yt_scr):
    tq = q_ref.shape[0]
    blk = SWA_WINDOW
    i = pl.program_id(1)
    lane_q = lax.broadcasted_iota(jnp.int32, (blk, D_GROUP), 1)
    head_of_col = lax.broadcasted_iota(jnp.int32, (1, SWA_HEADS * blk), 1) // blk
    sink_row = jnp.zeros((1, SWA_HEADS * blk), jnp.float32)
    for h in range(SWA_HEADS):
        sink_row = jnp.where(head_of_col == h, sinks_ref[h] * LOG2E, sink_row)
    def key_start(jb):
        return pl.multiple_of(jnp.maximum(i * tq + (jb - 1) * blk, 0), blk)

    def scores(jb):
        k_t = k_ref[pl.ds(key_start(jb), 2 * blk), :]
        q_b = q_ref[jb * blk:(jb + 1) * blk, :]
        q_stack = jnp.concatenate(
            [jnp.where((lane_q >= h * SWA_HEAD_DIM) & (lane_q < (h + 1) * SWA_HEAD_DIM), q_b, 0)
             for h in range(SWA_HEADS)], axis=0)
        return _dot_nt(k_t, q_stack)

    n_blk = tq // blk
    s_next = scores(0)
    for jb in range(n_blk):
        s_cur = s_next
        if jb + 1 < n_blk:
            s_next = scores(jb + 1)
        vt_t = vt_ref[:, pl.ds(key_start(jb), 2 * blk)]
        s = s_cur + bias_ref[jnp.minimum(i * tq + jb * blk, 1)]
        m = jnp.maximum(jnp.max(s, axis=0, keepdims=True), sink_row)
        if jb + 1 < n_blk:
            m = m + _zero_after(s_next[0:1, :])
        p = jnp.exp2(s - m)
        inv = 1.0 / (jnp.sum(p, axis=0, keepdims=True) + jnp.exp2(sink_row - m))
        o = _dot(vt_t, p.astype(jnp.bfloat16)) * inv
        for h in range(SWA_HEADS):
            yt_scr[h * SWA_HEAD_DIM:(h + 1) * SWA_HEAD_DIM, jb * blk:(jb + 1) * blk] = (
                o[(h // 2) * SWA_HEAD_DIM:(h // 2 + 1) * SWA_HEAD_DIM, h * blk:(h + 1) * blk])
    o_ref[...] = _rms(yt_scr[...].T, mix_ref[...]).astype(jnp.bfloat16)


def _swa_bias_tables(slopes):
    blk = SWA_WINDOW
    key = np.arange(2 * blk)[:, None]
    qry = np.arange(blk)[None, :]
    tabs = []
    for off in (0, blk):
        dist = off + qry - key
        valid = (dist >= 0) & (dist < SWA_WINDOW)
        tabs.append(np.concatenate([np.where(valid, -s * LOG2E * dist, NEG_BIG) for s in slopes], axis=1))
    return jnp.asarray(np.stack(tabs), jnp.float32)


def _swa_attention(sinks, q, k, vt, bias, mix_d, batch, seq):
    tq = 1024
    nq = seq // tq
    blk = SWA_WINDOW
    return pl.pallas_call(
        _swa_kernel,
        grid=(batch, nq),
        in_specs=[pl.BlockSpec(memory_space=pltpu.SMEM),
                  pl.BlockSpec((tq, D_GROUP), lambda b, i: (b * nq + i, 0)),
                  pl.BlockSpec((seq, D_GROUP), lambda b, i: (b, 0)),
                  pl.BlockSpec((None, SLOT, seq), lambda b, i: (b, 0, 0)),
                  pl.BlockSpec((2, 2 * blk, SWA_HEADS * blk), lambda b, i: (0, 0, 0)),
                  pl.BlockSpec((1, D_GROUP), lambda b, i: (0, 0))],
        out_specs=pl.BlockSpec((tq, D_GROUP), lambda b, i: (b * nq + i, 0)),
        out_shape=jax.ShapeDtypeStruct((batch * seq, D_GROUP), jnp.bfloat16),
        scratch_shapes=[pltpu.VMEM((D_GROUP, tq), jnp.float32)],
        compiler_params=pltpu.CompilerParams(dimension_semantics=("arbitrary", "arbitrary"),
                                             vmem_limit_bytes=40 * 1024 * 1024),
        name="swa_attention",
    )(sinks, q, k, vt, bias, mix_d)


def _out_stage_kernel(x_ref, ya_ref, yb_ref, yc_ref, yd_ref, w_o_ref, g_ffn_ref, w_gu_ref,
                      w_down_ref, g_final_ref, o_ref, act_scr, *, final):
    x = x_ref[...]
    for g, y_ref in enumerate((ya_ref, yb_ref, yc_ref, yd_ref)):
        x = x + _dot(y_ref[...], w_o_ref[g * D_GROUP:(g + 1) * D_GROUP, :])
    h2 = _rms(x, g_ffn_ref[...]).astype(jnp.bfloat16)

    for c in range(N_FF_CHUNKS):
        cols = slice(c * FF_CHUNK, (c + 1) * FF_CHUNK)
        gate = _dot(h2, w_gu_ref[:, cols])
        up = _dot(h2, w_gu_ref[:, D_FF + c * FF_CHUNK:D_FF + (c + 1) * FF_CHUNK])
        act_scr[:, cols] = (gate * jax.nn.sigmoid(gate) * up).astype(jnp.bfloat16)
    x = x + _dot(act_scr[...], w_down_ref[...])
    if final:
        x = _rms(x, g_final_ref[...])
    o_ref[...] = x


def _out_stage(x2d, ya, yb, yc, yd, lw, stacked, l, g_final, final):
    n_tok = x2d.shape[0]
    tm = TM_OUT
    const = lambda shape: pl.BlockSpec(shape, lambda t: (0,) * len(shape), pipeline_mode=pl.Buffered(1))
    of_layer = lambda shape: pl.BlockSpec((None,) + shape, lambda t: (l,) + (0,) * len(shape),
                                          pipeline_mode=pl.Buffered(1))
    tile = lambda w: pl.BlockSpec((tm, w), lambda t: (t, 0))
    return pl.pallas_call(
        functools.partial(_out_stage_kernel, final=final),
        grid=(n_tok // tm,),
        in_specs=[tile(D_MODEL), tile(D_GROUP), tile(D_GROUP), tile(D_GROUP), tile(D_GROUP),
                  of_layer((D_MODEL, D_MODEL)), const((1, D_MODEL)), of_layer((D_MODEL, 2 * D_FF)),
                  of_layer((D_FF, D_MODEL)), const((1, D_MODEL))],
        out_specs=tile(D_MODEL),
        out_shape=jax.ShapeDtypeStruct((n_tok, D_MODEL), jnp.float32),
        scratch_shapes=[pltpu.VMEM((tm, D_FF), jnp.bfloat16)],
        compiler_params=pltpu.CompilerParams(dimension_semantics=("arbitrary",),
                                             vmem_limit_bytes=52 * 1024 * 1024),
        name="out_stage",
    )(x2d, ya, yb, yc, yd, stacked["w_o"], lw["g_ffn"], stacked["w_gate_up"], stacked["w_down"], g_final)


def _rope_tables(seq):
    inv = 1.0 / (ROPE_THETA ** (jnp.arange(0, MLA_ROPE, 2, dtype=jnp.float32) / MLA_ROPE))
    ang = jnp.arange(seq, dtype=jnp.float32)[:, None] * inv[None, :]
    cos, sin = jnp.cos(ang), jnp.sin(ang)
    cos2 = jnp.concatenate([cos, cos], axis=1)
    sin2 = jnp.concatenate([sin, sin], axis=1)
    scale = LOG2E / math.sqrt(MLA_NOPE + MLA_ROPE)
    tq =jnp.concatenate([jnp.full((seq, MLA_NOPE), scale, jnp.float32), cos2 * scale, sin2 * scale], axis=1)
    tk = jnp.concatenate([jnp.zeros((seq, MLA_NOPE), jnp.float32), cos2, sin2], axis=1)
    return {"tq": tq, "tk": tk}


def _swap_halves(w):
    half = w.shape[-1] // 2
    return jnp.concatenate([-w[..., half:], w[..., :half]], axis=-1)


def _layer_weights(l, attn_norm, w_in, mla_q_norm, w_uq, mla_kv_norm, w_ukv, conv_w, pool_w, pool_scale,
                   mix_norm, ffn_norm):
    bf = jnp.bfloat16
    f32 = jnp.float32
    wi = w_in[l]
    pts = np.cumsum((0, 256, 128, 32, 256, 256, 256, 256, 256, 128, 128))
    c_q, c_kv, k_r, g_b, g_c, u_conv, u_pool, q_sw, k_sw, v_sw = [wi[:, pts[j]:pts[j + 1]] for j in range(10)]
    zeros = lambda w: jnp.zeros((D_MODEL, w), f32)
    w_in_r = jnp.concatenate([c_q, c_kv, zeros(MLA_NOPE), k_r, _swap_halves(k_r),
                              g_b, g_c, u_conv, u_pool, q_sw, k_sw, v_sw], axis=1)
    wq = w_uq[l].reshape(MLA_Q_RANK, MLA_HEADS, MLA_NOPE + MLA_ROPE)
    wq_rot = wq[..., MLA_NOPE:]
    w_uq_p = jnp.concatenate([wq, _swap_halves(wq_rot)], axis=-1).reshape(MLA_Q_RANK, MLA_W)
    wkv = w_ukv[l].reshape(MLA_KV_RANK, MLA_HEADS, MLA_NOPE + MLA_V)
    zk = jnp.zeros((MLA_KV_RANK, MLA_HEADS, SLOT - MLA_NOPE), f32)
    w_k = jnp.concatenate([wkv[..., :MLA_NOPE], zk], axis=-1).reshape(MLA_KV_RANK, MLA_W)
    zv = jnp.zeros((MLA_KV_RANK, MLA_HEADS, VT_ROWS - MLA_V), f32)
    w_v = jnp.concatenate([wkv[..., MLA_NOPE:], zv], axis=-1).reshape(MLA_KV_RANK, VT_ALL)
    w_v = jnp.concatenate([w_v, jnp.zeros((MLA_KV_RANK, VT_PAD - VT_ALL), f32)], axis=1)
    w_pool = jax.scipy.linalg.block_diag(*[pool_w[l, g] for g in range(len(POOL_WINDOWS))])
    mix = mix_norm[l].reshape(4, 1, D_GROUP)
    return {
        "g_attn": attn_norm[l][None, :], "w_in": w_in_r.astype(bf),
        "g_q": mla_q_norm[l][None, :], "w_uq": w_uq_p.astype(bf),
        "g_kv": mla_kv_norm[l][None, :], "w_kv": jnp.concatenate([w_k, w_v], axis=1).astype(bf),
        "conv_w": conv_w[l], "w_pool": w_pool.astype(bf), "pool_scale": pool_scale[l][None, :],
        "mix_a": mix[0], "mix_b": mix[1], "mix_c": mix[2], "mix_d": mix[3],
        "g_ffn": ffn_norm[l][None, :],
    }


def kernel(x, attn_norm, w_in, mla_q_norm, w_uq, mla_kv_norm, w_ukv, conv_w, pool_w, pool_scale, swa_sinks,
           mix_norm, w_o, ffn_norm, w_gate_up, w_down, final_norm):
    batch, seq, d_model = x.shape
    depth = w_in.shape[0]
    assert d_model == D_MODEL and w_in.shape[2] == D_IN
    assert seq % TM_IN == 0 and seq % TQ == 0 and TQ % TK == 0 and (batch * seq) % TM_OUT == 0
    slopes = tuple(float(2.0 ** (-8.0 * (h + 1) / SWA_HEADS)) for h in range(SWA_HEADS))
    tabs = _rope_tables(seq)
    swa_bias = _swa_bias_tables(slopes)
    x2d = x.reshape(batch * seq, D_MODEL)
    g_final = final_norm[None, :]
    stacked = {"w_o": w_o.astype(jnp.bfloat16), "w_gate_up": w_gate_up.astype(jnp.bfloat16),
               "w_down": w_down.astype(jnp.bfloat16)}
    for l in range(depth):
        lw = _layer_weights(l, attn_norm, w_in, mla_q_norm, w_uq, mla_kv_norm, w_ukv, conv_w, pool_w,
                            pool_scale, mix_norm, ffn_norm)
        q, k, vt, yb, yc, qsw, ksw, vswt = _in_stage(x2d, lw, tabs, seq)
        ya = _mla_attention(q, k, vt, lw["mix_a"], batch, seq)
        yd = _swa_attention(swa_sinks[l], qsw, ksw, vswt, swa_bias, lw["mix_d"], batch, seq)
        x2d = _out_stage(x2d, ya, yb, yc, yd, lw, stacked, l, g_final, final=(l == depth - 1))
    return x2d.reshape(batch, seq, D_MODEL)
```

```python
import functools
import math

import jax
import jax.numpy as jnp
import numpy as np
from jax import lax
from jax.experimental import pallas as pl
from jax.experimental.pallas import tpu as pltpu

D_MODEL = 1024
D_GROUP = 256
MLA_HEADS = 4
MLA_Q_RANK = 256
MLA_KV_RANK = 128
MLA_NOPE = 64
MLA_ROPE = 32
MLA_V = 64
ROPE_THETA = 10000.0
CONV_WIDTH = 3
POOL_WINDOWS = (2, 4, 8, 16)
POOL_CH = 64
SWA_HEADS = 4
SWA_KV_HEADS = 2
SWA_HEAD_DIM = 64
SWA_WINDOW = 128
D_FF = 2816
RMS_EPS = 1e-6
D_IN = 1952

LANES = 128
V7X_VMEM_BYTES = 64 * 1024 * 1024

D_IN_PAD = 2048
OFF_CQ, OFF_CKV, OFF_KR, OFF_GB, OFF_GC, OFF_UCONV, OFF_UPOOL = 0, 256, 384, 512, 768, 1024, 1280
OFF_QSW, OFF_KSW, OFF_VSW = 1536, 1792, 1920
IN_GROUPS = ((0, D_IN_PAD),)
SLOT = LANES
MLA_W = MLA_HEADS * SLOT
VT_ROWS = 80
VT_ALL = MLA_HEADS * VT_ROWS
VT_PAD = 384
LOG2E = math.log2(math.e)
POOL_HDR = 32
CONV_HDR = 8
NEG_BIG = -1e30

TM_IN = 1024
TQ = 1024
TK = 512
TM_OUT = 1024
FF_CHUNK = 256
N_FF_CHUNKS = D_FF // FF_CHUNK


def _rms(x, g):
    return x * lax.rsqrt(jnp.mean(x * x, axis=-1, keepdims=True) + RMS_EPS) * g


def _dot(a, b):
    return lax.dot_general(a, b, (((1,), (0,)), ((), ())), preferred_element_type=jnp.float32)


def _zero_after(x):
    bits = pltpu.bitcast(x, jnp.int32)
    return lax.shift_right_logical(lax.shift_right_logical(bits, 16), 16).astype(jnp.float32)


def _dot_nt(a, b):
    return lax.dot_general(a, b, (((1,), (1,)), ((), ())), preferred_element_type=jnp.float32)


def _in_stage_kernel(x_ref, g_attn_ref, w_in_ref, g_q_ref, w_uq_ref, g_kv_ref, w_kv_ref,
                     tq_ref, tk_ref, conv_w_ref, w_pool_ref, pool_scale_ref, mixb_ref, mixc_ref,
                     q_ref, k_ref, vt_ref, yb_ref, yc_ref, qsw_ref, ksw_ref, vswt_ref,
                     conv_scr, p0, p1, p2, *, tiles_per_seq):
    tm = x_ref.shape[0]
    t = pl.program_id(0)
    tile_in_seq = t % tiles_per_seq
    pos0 = pl.multiple_of(tile_in_seq * tm, tm)

    @pl.when(tile_in_seq == 0)
    def _():
        conv_scr[0:CONV_HDR, :] = jnp.zeros((CONV_HDR, D_GROUP), jnp.float32)
        p0[0:POOL_HDR, :] = jnp.zeros((POOL_HDR, D_GROUP), jnp.float32)

    h = _rms(x_ref[...], g_attn_ref[...]).astype(jnp.bfloat16)
    group_vals = {}

    def proj(off, width):
        lo, hi = next(g for g in IN_GROUPS if g[0] <= off < g[1])
        if lo not in group_vals:
            group_vals[lo] = _dot(h, w_in_ref[:, lo:hi])
        return group_vals[lo][:, off - lo:off - lo + width]

    lane = lax.broadcasted_iota(jnp.int32, (tm, MLA_W), 1) % SLOT
    qn = _rms(proj(OFF_CQ, MLA_Q_RANK), g_q_ref[...]).astype(jnp.bfloat16)
    qa = _dot(qn, w_uq_ref[...])
    tq_tab = tq_ref[pl.ds(pos0, tm), :]
    qp = qa * jnp.concatenate([tq_tab] * MLA_HEADS, axis=1)
    q_rot = pltpu.roll(qp, MLA_W - MLA_ROPE, axis=1)
    q = jnp.where(lane < MLA_NOPE + MLA_ROPE, qp, 0.0) + jnp.where(
        (lane >= MLA_NOPE) & (lane < MLA_NOPE + MLA_ROPE), q_rot, 0.0)
    q_ref[...] = q.astype(jnp.bfloat16)

    ckn = _rms(proj(OFF_CKV, MLA_KV_RANK), g_kv_ref[...]).astype(jnp.bfloat16)
    kv = _dot(ckn, w_kv_ref[...])
    kr = proj(OFF_KR, SLOT) * tk_ref[pl.ds(pos0, tm), :]
    lane1 = lax.broadcasted_iota(jnp.int32, (tm, SLOT), 1)
    kr = jnp.where((lane1 >= MLA_NOPE) & (lane1 < MLA_NOPE + MLA_ROPE),
                   kr + pltpu.roll(kr, SLOT - MLA_ROPE, axis=1), 0.0)
    k_ref[...] = (kv[:, :MLA_W] + jnp.concatenate([kr] * MLA_HEADS, axis=1)).astype(jnp.bfloat16)

    z = proj(OFF_GC, D_GROUP) * proj(OFF_UCONV, D_GROUP)
    conv_scr[CONV_HDR:CONV_HDR + tm, :] = z
    z1 = conv_scr[CONV_HDR - 1:CONV_HDR - 1 + tm, :]
    z2 = conv_scr[CONV_HDR - 2:CONV_HDR - 2 + tm, :]
    cw = conv_w_ref[...]
    y_b = proj(OFF_GB, D_GROUP) * (cw[0:1, :] * z2 + cw[1:2, :] * z1 + cw[2:3, :] * z)
    conv_scr[0:CONV_HDR, :] = z[tm - CONV_HDR:tm, :]
    yb_ref[...] = _rms(y_b, mixb_ref[...]).astype(jnp.bfloat16)

    u = proj(OFF_UPOOL, D_GROUP)
    n = tm + POOL_HDR
    p0[POOL_HDR:n, :] = u
    p1[8:n, :] = p0[8:n, :] + p0[7:n - 1, :]
    s2 = p1[POOL_HDR:n, :]
    p2[16:n, :] = p1[16:n, :] + p1[14:n - 2, :]
    s4 = p2[POOL_HDR:n, :]
    p1[24:n, :] = p2[24:n, :] + p2[20:n - 4, :]
    s8 = p1[POOL_HDR:n, :]
    s16 = s8 + p1[24:n - 8, :]
    p0[0:POOL_HDR, :] = u[tm - POOL_HDR:tm, :]
    lane_c = lax.broadcasted_iota(jnp.int32, (tm, D_GROUP), 1)
    row_c = lax.broadcasted_iota(jnp.int32, (tm, D_GROUP), 0)
    win = jnp.where(lane_c < POOL_CH, s2, jnp.where(lane_c < 2 * POOL_CH, s4,
                    jnp.where(lane_c < 3 * POOL_CH, s8, s16)))
    width = jnp.where(lane_c < POOL_CH, POOL_WINDOWS[0], jnp.where(lane_c < 2 * POOL_CH, POOL_WINDOWS[1],
                      jnp.where(lane_c < 3 * POOL_CH, POOL_WINDOWS[2], POOL_WINDOWS[3])))
    count = jnp.minimum(pos0 + row_c + 1, width).astype(jnp.float32)
    pooled = win / count - u
    y_c = _dot(pooled.astype(jnp.bfloat16), w_pool_ref[...]) * pool_scale_ref[...]
    yc_ref[...] = _rms(y_c, mixc_ref[...]).astype(jnp.bfloat16)

    qsw_ref[...] = (proj(OFF_QSW, D_GROUP) * (LOG2E / math.sqrt(SWA_HEAD_DIM))).astype(jnp.bfloat16)
    lane_s = lax.broadcasted_iota(jnp.int32, (tm, SLOT), 1)
    a = proj(OFF_KSW, SLOT)
    r = pltpu.roll(a, SWA_HEAD_DIM, axis=1)
    ksw_ref[...] = jnp.concatenate([jnp.where(lane_s < SWA_HEAD_DIM, a, r),
                                    jnp.where(lane_s < SWA_HEAD_DIM, r, a)], axis=1).astype(jnp.bfloat16)

    lane_v = lax.broadcasted_iota(jnp.int32, (tm, VT_PAD), 1)
    ones_col = jnp.where((lane_v % VT_ROWS == MLA_V) & (lane_v < VT_ALL), 1.0, 0.0)
    v_t = jnp.concatenate([proj(OFF_VSW, SLOT), kv[:, MLA_W:] + ones_col], axis=1).T
    vswt_ref[...] = v_t[:SLOT, :].astype(jnp.bfloat16)
    vt_ref[...] = v_t[SLOT:SLOT + VT_ALL, :].astype(jnp.bfloat16)


def _in_stage(x2d, lw, tabs, seq):
    n_tok = x2d.shape[0]
    tm = TM_IN
    tiles_per_seq = seq // tm
    const = lambda shape: pl.BlockSpec(shape, lambda t: (0, 0), pipeline_mode=pl.Buffered(1))
    tile = lambda w: pl.BlockSpec((tm, w), lambda t: (t, 0))
    vt_spec = lambda rows: pl.BlockSpec((None, rows, tm), lambda t: (t // tiles_per_seq, 0, t % tiles_per_seq))
    bf = jnp.bfloat16
    tok = lambda w: jax.ShapeDtypeStruct((n_tok, w), bf)
    out_shape = [tok(MLA_W), tok(MLA_W), jax.ShapeDtypeStruct((n_tok // seq, VT_ALL, seq), bf),
                 tok(D_GROUP), tok(D_GROUP), tok(D_GROUP), tok(D_GROUP),
                 jax.ShapeDtypeStruct((n_tok // seq, SLOT, seq), bf)]
    return pl.pallas_call(
        functools.partial(_in_stage_kernel, tiles_per_seq=tiles_per_seq),
        grid=(n_tok // tm,),
        in_specs=[tile(D_MODEL), const((1, D_MODEL)), const((D_MODEL, D_IN_PAD)),
                  const((1, MLA_Q_RANK)), const((MLA_Q_RANK, MLA_W)),
                  const((1, MLA_KV_RANK)), const((MLA_KV_RANK, MLA_W + VT_PAD)),
                  const((seq, SLOT)), const((seq, SLOT)),
                  const((CONV_WIDTH, D_GROUP)), const((D_GROUP, D_GROUP)), const((1, D_GROUP)),
                  const((1, D_GROUP)), const((1, D_GROUP))],
        out_specs=[tile(MLA_W), tile(MLA_W), vt_spec(VT_ALL), tile(D_GROUP), tile(D_GROUP),
                   tile(D_GROUP), tile(D_GROUP), vt_spec(SLOT)],
        out_shape=out_shape,
        scratch_shapes=[pltpu.VMEM((tm + CONV_HDR, D_GROUP), jnp.float32),
                        pltpu.VMEM((tm + POOL_HDR, D_GROUP), jnp.float32),
                        pltpu.VMEM((tm + POOL_HDR, D_GROUP), jnp.float32),
                        pltpu.VMEM((tm + POOL_HDR, D_GROUP), jnp.float32)],
        compiler_params=pltpu.CompilerParams(dimension_semantics=("arbitrary",),
                                             vmem_limit_bytes=40 * 1024 * 1024),
        name="in_stage",
    )(x2d, lw["g_attn"], lw["w_in"], lw["g_q"], lw["w_uq"], lw["g_kv"], lw["w_kv"],
      tabs["tq"], tabs["tk"], lw["conv_w"], lw["w_pool"], lw["pool_scale"], lw["mix_b"], lw["mix_c"])


def _mla_kernel(q_ref, k_ref, vt_ref, mix_ref, o_ref, acc_scr, s_scr):
    tq = q_ref.shape[0]
    i = pl.program_id(1)
    acc_scr[...] = jnp.zeros(acc_scr.shape, jnp.float32)

    def scores(h, cols, key_start):
        k_h = k_ref[pl.ds(key_start, TK), h * SLOT:(h + 1) * SLOT]
        return _dot_nt(k_h, q_ref[cols, h * SLOT:(h + 1) * SLOT])

    def accumulate(h, s, m_old, cols, key_start, mask):
        if mask is not None:
            s = jnp.where(mask, s, NEG_BIG)
        m_new = jnp.maximum(m_old, jnp.max(s, axis=0, keepdims=True))
        p = jnp.exp2(s - m_new).astype(jnp.bfloat16)
        vt_h = vt_ref[h * VT_ROWS:(h + 1) * VT_ROWS, pl.ds(key_start, TK)]
        acc_scr[h, :, cols] = jnp.exp2(m_old - m_new) * acc_scr[h, :, cols] + _dot(vt_h, p)
        return m_new

    def run_units(units, m, following):
        m = list(m)
        s_next = s_scr[...]
        for u, (h, cols, key_start, mask) in enumerate(units):
            s_cur = s_next
            nxt = units[u + 1] if u + 1 < len(units) else following
            if nxt is not None:
                s_next = scores(*nxt[:3])
            ncol = cols.stop - cols.start
            m_new = accumulate(h, s_cur, m[h][:, m[h].shape[1] - ncol:], cols, key_start, mask)
            m[h] = m_new
        if following is not None:
            s_scr[...] = s_next
        return tuple(m)

    all_cols = slice(0, tq)
    tiles_per_step = tq // TK
    tile_start = lambda t: pl.multiple_of(t * TK, TK)

    def full_tiles(j, m):
        units = []
        for t in range(tiles_per_step):
            units += [(h, all_cols, tile_start(j * tiles_per_step + t), None) for h in range(MLA_HEADS)]
        return run_units(units, m, following=(0, all_cols, tile_start((j + 1) * tiles_per_step)))

    s_scr[...] = scores(0, all_cols, 0)
    m = tuple(jnp.full((1, tq), NEG_BIG, jnp.float32) for _ in range(MLA_HEADS))
    m = lax.fori_loop(0, i, full_tiles, m)

    units = []
    for d in range(tiles_per_step):
        key_start = pl.multiple_of(i * tq + d * TK, TK)
        ncol = tq - d * TK
        mask = (lax.broadcasted_iota(jnp.int32, (TK, ncol), 0) <= lax.broadcasted_iota(jnp.int32, (TK, ncol), 1))
        units += [(h, slice(d * TK, tq), key_start, mask) for h in range(MLA_HEADS)]
    run_units(units, m, following=None)

    y_t = jnp.concatenate([acc_scr[h, 0:MLA_V, :] / acc_scr[h, MLA_V:MLA_V + 1, :] for h in range(MLA_HEADS)], axis=0)
    o_ref[...] = _rms(y_t.T, mix_ref[...]).astype(jnp.bfloat16)


def _mla_attention(q, k, vt, mix_a, batch, seq):
    nq = seq // TQ
    return pl.pallas_call(
        _mla_kernel,
        grid=(batch, nq),
        in_specs=[pl.BlockSpec((TQ, MLA_W), lambda b, i: (b * nq + i, 0)),
                  pl.BlockSpec((seq, MLA_W), lambda b, i: (b, 0)),
                  pl.BlockSpec((None, VT_ALL, seq), lambda b, i: (b, 0, 0)),
                  pl.BlockSpec((1, D_GROUP), lambda b, i: (0, 0))],
        out_specs=pl.BlockSpec((TQ, D_GROUP), lambda b, i: (b * nq + i, 0)),
        out_shape=jax.ShapeDtypeStruct((batch * seq, D_GROUP), jnp.bfloat16),
        scratch_shapes=[pltpu.VMEM((MLA_HEADS, VT_ROWS, TQ), jnp.float32),
                        pltpu.VMEM((TK, TQ), jnp.float32)],
        compiler_params=pltpu.CompilerParams(dimension_semantics=("arbitrary", "arbitrary"),
                                             vmem_limit_bytes=40 * 1024 * 1024),
        name="mla_attention",
    )(q, k, vt, mix_a)


def _swa_kernel(sinks_ref, q_ref, k_ref, vt_ref, bias_ref, mix_ref, o_ref, yt_scr):
    tq = q_ref.shape[0]
    blk = SWA_WINDOW
    i = pl.program_id(1)
    lane_q = lax.broadcasted_iota(jnp.int32, (blk, D_GROUP), 1)
    head_of_col = lax.broadcasted_iota(jnp.int32, (1, SWA_HEADS * blk), 1) // blk
    sink_row = jnp.zeros((1, SWA_HEADS * blk), jnp.float32)
    for h in range(SWA_HEADS):
        sink_row = jnp.where(head_of_col == h, sinks_ref[h] * LOG2E, sink_row)
    def key_start(jb):
        return pl.multiple_of(jnp.maximum(i * tq + (jb - 1) * blk, 0), blk)

    def scores(jb):
        k_t = k_ref[pl.ds(key_start(jb), 2 * blk), :]
        q_b = q_ref[jb * blk:(jb + 1) * blk, :]
        q_stack = jnp.concatenate(
            [jnp.where((lane_q >= h * SWA_HEAD_DIM) & (lane_q < (h + 1) * SWA_HEAD_DIM), q_b, 0)
             for h in range(SWA_HEADS)], axis=0)
        return _dot_nt(k_t, q_stack)

    n_blk = tq // blk
    s_next = scores(0)
    for jb in range(n_blk):
        s_cur = s_next
        if jb + 1 < n_blk:
            s_next = scores(jb + 1)
        vt_t = vt_ref[:, pl.ds(key_start(jb), 2 * blk)]
        s = s_cur + bias_ref[jnp.minimum(i * tq + jb * blk, 1)]
        m = jnp.maximum(jnp.max(s, axis=0, keepdims=True), sink_row)
        if jb + 1 < n_blk:
            m = m + _zero_after(s_next[0:1, :])
        p = jnp.exp2(s - m)
        inv = 1.0 / (jnp.sum(p, axis=0, keepdims=True) + jnp.exp2(sink_row - m))
        o = _dot(vt_t, p.astype(jnp.bfloat16)) * inv
        for h in range(SWA_HEADS):
            yt_scr[h * SWA_HEAD_DIM:(h + 1) * SWA_HEAD_DIM, jb * blk:(jb + 1) * blk] = (
                o[(h // 2) * SWA_HEAD_DIM:(h // 2 + 1) * SWA_HEAD_DIM, h * blk:(h + 1) * blk])
    o_ref[...] = _rms(yt_scr[...].T, mix_ref[...]).astype(jnp.bfloat16)


def _swa_bias_tables(slopes):
    blk = SWA_WINDOW
    key = np.arange(2 * blk)[:, None]
    qry = np.arange(blk)[None, :]
    tabs = []
    for off in (0, blk):
        dist = off + qry - key
        valid = (dist >= 0) & (dist < SWA_WINDOW)
        tabs.append(np.concatenate([np.where(valid, -s * LOG2E * dist, NEG_BIG) for s in slopes], axis=1))
    return jnp.asarray(np.stack(tabs), jnp.float32)


def _swa_attention(sinks, q, k, vt, bias, mix_d, batch, seq):
    tq = 1024
    nq = seq // tq
    blk = SWA_WINDOW
    return pl.pallas_call(
        _swa_kernel,
        grid=(batch, nq),
        in_specs=[pl.BlockSpec(memory_space=pltpu.SMEM),
                  pl.BlockSpec((tq, D_GROUP), lambda b, i: (b * nq + i, 0)),
                  pl.BlockSpec((seq, D_GROUP), lambda b, i: (b, 0)),
                  pl.BlockSpec((None, SLOT, seq), lambda b, i: (b, 0, 0)),
                  pl.BlockSpec((2, 2 * blk, SWA_HEADS * blk), lambda b, i: (0, 0, 0)),
                  pl.BlockSpec((1, D_GROUP), lambda b, i: (0, 0))],
        out_specs=pl.BlockSpec((tq, D_GROUP), lambda b, i: (b * nq + i, 0)),
        out_shape=jax.ShapeDtypeStruct((batch * seq, D_GROUP), jnp.bfloat16),
        scratch_shapes=[pltpu.VMEM((D_GROUP, tq), jnp.float32)],
        compiler_params=pltpu.CompilerParams(dimension_semantics=("arbitrary", "arbitrary"),
                                             vmem_limit_bytes=40 * 1024 * 1024),
        name="swa_attention",
    )(sinks, q, k, vt, bias, mix_d)


def _out_stage_kernel(x_ref, ya_ref, yb_ref, yc_ref, yd_ref, w_o_ref, g_ffn_ref, w_gu_ref,
                      w_down_ref, g_final_ref, o_ref, act_scr, *, final):
    x = x_ref[...]
    for g, y_ref in enumerate((ya_ref, yb_ref, yc_ref, yd_ref)):
        x = x + _dot(y_ref[...], w_o_ref[g * D_GROUP:(g + 1) * D_GROUP, :])
    h2 = _rms(x, g_ffn_ref[...]).astype(jnp.bfloat16)

    for c in range(N_FF_CHUNKS):
        cols = slice(c * FF_CHUNK, (c + 1) * FF_CHUNK)
        gate = _dot(h2, w_gu_ref[:, cols])
        up = _dot(h2, w_gu_ref[:, D_FF + c * FF_CHUNK:D_FF + (c + 1) * FF_CHUNK])
        act_scr[:, cols] = (gate * jax.nn.sigmoid(gate) * up).astype(jnp.bfloat16)
    x = x + _dot(act_scr[...], w_down_ref[...])
    if final:
        x = _rms(x, g_final_ref[...])
    o_ref[...] = x


def _out_stage(x2d, ya, yb, yc, yd, lw, stacked, l, g_final, final):
    n_tok = x2d.shape[0]
    tm = TM_OUT
    const = lambda shape: pl.BlockSpec(shape, lambda t: (0,) * len(shape), pipeline_mode=pl.Buffered(1))
    of_layer = lambda shape: pl.BlockSpec((None,) + shape, lambda t: (l,) + (0,) * len(shape),
                                          pipeline_mode=pl.Buffered(1))
    tile = lambda w: pl.BlockSpec((tm, w), lambda t: (t, 0))
    return pl.pallas_call(
        functools.partial(_out_stage_kernel, final=final),
        grid=(n_tok // tm,),
        in_specs=[tile(D_MODEL), tile(D_GROUP), tile(D_GROUP), tile(D_GROUP), tile(D_GROUP),
                  of_layer((D_MODEL, D_MODEL)), const((1, D_MODEL)), of_layer((D_MODEL, 2 * D_FF)),
                  of_layer((D_FF, D_MODEL)), const((1, D_MODEL))],
        out_specs=tile(D_MODEL),
        out_shape=jax.ShapeDtypeStruct((n_tok, D_MODEL), jnp.float32),
        scratch_shapes=[pltpu.VMEM((tm, D_FF), jnp.bfloat16)],
        compiler_params=pltpu.CompilerParams(dimension_semantics=("arbitrary",),
                                             vmem_limit_bytes=52 * 1024 * 1024),
        name="out_stage",
    )(x2d, ya, yb, yc, yd, stacked["w_o"], lw["g_ffn"], stacked["w_gate_up"], stacked["w_down"], g_final)


def _rope_tables(seq):
    inv = 1.0 / (ROPE_THETA ** (jnp.arange(0, MLA_ROPE, 2, dtype=jnp.float32) / MLA_ROPE))
    ang = jnp.arange(seq, dtype=jnp.float32)[:, None] * inv[None, :]
    cos, sin = jnp.cos(ang), jnp.sin(ang)
    cos2 = jnp.concatenate([cos, cos], axis=1)
    sin2 = jnp.concatenate([sin, sin], axis=1)
    scale = LOG2E / math.sqrt(MLA_NOPE + MLA_ROPE)
    tq =jnp.concatenate([jnp.full((seq, MLA_NOPE), scale, jnp.float32), cos2 * scale, sin2 * scale], axis=1)
    tk = jnp.concatenate([jnp.zeros((seq, MLA_NOPE), jnp.float32), cos2, sin2], axis=1)
    return {"tq": tq, "tk": tk}


def _swap_halves(w):
    half = w.shape[-1] // 2
    return jnp.concatenate([-w[..., half:], w[..., :half]], axis=-1)


def _layer_weights(l, attn_norm, w_in, mla_q_norm, w_uq, mla_kv_norm, w_ukv, conv_w, pool_w, pool_scale,
                   mix_norm, ffn_norm):
    bf = jnp.bfloat16
    f32 = jnp.float32
    wi = w_in[l]
    pts = np.cumsum((0, 256, 128, 32, 256, 256, 256, 256, 256, 128, 128))
    c_q, c_kv, k_r, g_b, g_c, u_conv, u_pool, q_sw, k_sw, v_sw = [wi[:, pts[j]:pts[j + 1]] for j in range(10)]
    zeros = lambda w: jnp.zeros((D_MODEL, w), f32)
    w_in_r = jnp.concatenate([c_q, c_kv, zeros(MLA_NOPE), k_r, _swap_halves(k_r),
                              g_b, g_c, u_conv, u_pool, q_sw, k_sw, v_sw], axis=1)
    wq = w_uq[l].reshape(MLA_Q_RANK, MLA_HEADS, MLA_NOPE + MLA_ROPE)
    wq_rot = wq[..., MLA_NOPE:]
    w_uq_p = jnp.concatenate([wq, _swap_halves(wq_rot)], axis=-1).reshape(MLA_Q_RANK, MLA_W)
    wkv = w_ukv[l].reshape(MLA_KV_RANK, MLA_HEADS, MLA_NOPE + MLA_V)
    zk = jnp.zeros((MLA_KV_RANK, MLA_HEADS, SLOT - MLA_NOPE), f32)
    w_k = jnp.concatenate([wkv[..., :MLA_NOPE], zk], axis=-1).reshape(MLA_KV_RANK, MLA_W)
    zv = jnp.zeros((MLA_KV_RANK, MLA_HEADS, VT_ROWS - MLA_V), f32)
    w_v = jnp.concatenate([wkv[..., MLA_NOPE:], zv], axis=-1).reshape(MLA_KV_RANK, VT_ALL)
    w_v = jnp.concatenate([w_v, jnp.zeros((MLA_KV_RANK, VT_PAD - VT_ALL), f32)], axis=1)
    w_pool = jax.scipy.linalg.block_diag(*[pool_w[l, g] for g in range(len(POOL_WINDOWS))])
    mix = mix_norm[l].reshape(4, 1, D_GROUP)
    return {
        "g_attn": attn_norm[l][None, :], "w_in": w_in_r.astype(bf),
        "g_q": mla_q_norm[l][None, :], "w_uq": w_uq_p.astype(bf),
        "g_kv": mla_kv_norm[l][None, :], "w_kv": jnp.concatenate([w_k, w_v], axis=1).astype(bf),
        "conv_w": conv_w[l], "w_pool": w_pool.astype(bf), "pool_scale": pool_scale[l][None, :],
        "mix_a": mix[0], "mix_b": mix[1], "mix_c": mix[2], "mix_d": mix[3],
        "g_ffn": ffn_norm[l][None, :],
    }


def kernel(x, attn_norm, w_in, mla_q_norm, w_uq, mla_kv_norm, w_ukv, conv_w, pool_w, pool_scale, swa_sinks,
           mix_norm, w_o, ffn_norm, w_gate_up, w_down, final_norm):
    batch, seq, d_model = x.shape
    depth = w_in.shape[0]
    assert d_model == D_MODEL and w_in.shape[2] == D_IN
    assert seq % TM_IN == 0 and seq % TQ == 0 and TQ % TK == 0 and (batch * seq) % TM_OUT == 0
    slopes = tuple(float(2.0 ** (-8.0 * (h + 1) / SWA_HEADS)) for h in range(SWA_HEADS))
    tabs = _rope_tables(seq)
    swa_bias = _swa_bias_tables(slopes)
    x2d = x.reshape(batch * seq, D_MODEL)
    g_final = final_norm[None, :]
    stacked = {"w_o": w_o.astype(jnp.bfloat16), "w_gate_up": w_gate_up.astype(jnp.bfloat16),
               "w_down": w_down.astype(jnp.bfloat16)}
    for l in range(depth):
        lw = _layer_weights(l, attn_norm, w_in, mla_q_norm, w_uq, mla_kv_norm, w_ukv, conv_w, pool_w,
                            pool_scale, mix_norm, ffn_norm)
        q, k, vt, yb, yc, qsw, ksw, vswt = _in_stage(x2d, lw, tabs, seq)
        ya = _mla_attention(q, k, vt, lw["mix_a"], batch, seq)
        yd = _swa_attention(swa_sinks[l], qsw, ksw, vswt, swa_bias, lw["mix_d"], batch, seq)
        x2d = _out_stage(x2d, ya, yb, yc, yd, lw, stacked, l, g_final, final=(l == depth - 1))
    return x2d.reshape(batch, seq, D_MODEL)
```

```python
import functools
import math

import jax
import jax.numpy as jnp
import numpy as np
from jax import lax
from jax.experimental import pallas as pl
from jax.experimental.pallas import tpu as pltpu

D_MODEL = 1024
D_GROUP = 256
MLA_HEADS = 4
MLA_Q_RANK = 256
MLA_KV_RANK = 128
MLA_NOPE = 64
MLA_ROPE = 32
MLA_V = 64
ROPE_THETA = 10000.0
CONV_WIDTH = 3
POOL_WINDOWS = (2, 4, 8, 16)
POOL_CH = 64
SWA_HEADS = 4
SWA_KV_HEADS = 2
SWA_HEAD_DIM = 64
SWA_WINDOW = 128
D_FF = 2816
RMS_EPS = 1e-6
D_IN = 1952

LANES = 128
MIB = 1024 * 1024
VMEM_LIMIT_IN_STAGE = 40 * MIB
VMEM_LIMIT_ATTENTION = 40 * MIB
VMEM_LIMIT_OUT_STAGE = 52 * MIB

D_IN_PAD = 2048
OFF_CQ, OFF_CKV, OFF_KR, OFF_GB, OFF_GC, OFF_UCONV, OFF_UPOOL = 0, 256, 384, 512, 768, 1024, 1280
OFF_QSW, OFF_KSW, OFF_VSW = 1536, 1792, 1920
SLOT = LANES
MLA_W = MLA_HEADS * SLOT
VT_ROWS = 80
VT_ALL = MLA_HEADS * VT_ROWS
VT_PAD = 384
LOG2E = math.log2(math.e)
POOL_HDR = 32
CONV_HDR = 8
NEG_BIG = -1e30

TM_IN = 1024
TQ = 1024
TK = 512
TM_OUT = 1024
FF_CHUNK = 256
N_FF_CHUNKS = D_FF // FF_CHUNK


def _rms(x, g):
    return x * lax.rsqrt(jnp.mean(x * x, axis=-1, keepdims=True) + RMS_EPS) * g


def _dot(a, b):
    return lax.dot_general(a, b, (((1,), (0,)), ((), ())), preferred_element_type=jnp.float32)


def _zero_after(x):
    bits = pltpu.bitcast(x, jnp.int32)
    return lax.shift_right_logical(lax.shift_right_logical(bits, 16), 16).astype(jnp.float32)


def _dot_nt(a, b):
    return lax.dot_general(a, b, (((1,), (1,)), ((), ())), preferred_element_type=jnp.float32)


def _in_stage_kernel(x_ref, g_attn_ref, w_in_ref, g_q_ref, w_uq_ref, g_kv_ref, w_kv_ref,
                     tq_ref, tk_ref, conv_w_ref, w_pool_ref, pool_scale_ref, mixb_ref, mixc_ref,
                     q_ref, k_ref, vt_ref, yb_ref, yc_ref, qsw_ref, ksw_ref, vswt_ref,
                     conv_scr, p0, p1, p2, *, tiles_per_seq):
    tm = x_ref.shape[0]
    t = pl.program_id(0)
    tile_in_seq = t % tiles_per_seq
    pos0 = pl.multiple_of(tile_in_seq * tm, tm)

    @pl.when(tile_in_seq == 0)
    def _():
        conv_scr[0:CONV_HDR, :] = jnp.zeros((CONV_HDR, D_GROUP), jnp.float32)
        p0[0:POOL_HDR, :] = jnp.zeros((POOL_HDR, D_GROUP), jnp.float32)

    h = _rms(x_ref[...], g_attn_ref[...]).astype(jnp.bfloat16)
    proj_all = _dot(h, w_in_ref[...])
    proj = lambda off, width: proj_all[:, off:off + width]

    lane = lax.broadcasted_iota(jnp.int32, (tm, MLA_W), 1) % SLOT
    qn = _rms(proj(OFF_CQ, MLA_Q_RANK), g_q_ref[...]).astype(jnp.bfloat16)
    qa = _dot(qn, w_uq_ref[...])
    tq_tab = tq_ref[pl.ds(pos0, tm), :]
    qp = qa * jnp.concatenate([tq_tab] * MLA_HEADS, axis=1)
    q_rot = pltpu.roll(qp, MLA_W - MLA_ROPE, axis=1)
    q = jnp.where(lane < MLA_NOPE + MLA_ROPE, qp, 0.0) + jnp.where(
        (lane >= MLA_NOPE) & (lane < MLA_NOPE + MLA_ROPE), q_rot, 0.0)
    q_ref[...] = q.astype(jnp.bfloat16)

    ckn = _rms(proj(OFF_CKV, MLA_KV_RANK), g_kv_ref[...]).astype(jnp.bfloat16)
    kv = _dot(ckn, w_kv_ref[...])
    kr = proj(OFF_KR, SLOT) * tk_ref[pl.ds(pos0, tm), :]
    lane1 = lax.broadcasted_iota(jnp.int32, (tm, SLOT), 1)
    kr = jnp.where((lane1 >= MLA_NOPE) & (lane1 < MLA_NOPE + MLA_ROPE),
                   kr + pltpu.roll(kr, SLOT - MLA_ROPE, axis=1), 0.0)
    k_ref[...] = (kv[:, :MLA_W] + jnp.concatenate([kr] * MLA_HEADS, axis=1)).astype(jnp.bfloat16)

    z = proj(OFF_GC, D_GROUP) * proj(OFF_UCONV, D_GROUP)
    conv_scr[CONV_HDR:CONV_HDR + tm, :] = z
    z1 = conv_scr[CONV_HDR - 1:CONV_HDR - 1 + tm, :]
    z2 = conv_scr[CONV_HDR - 2:CONV_HDR - 2 + tm, :]
    cw = conv_w_ref[...]
    y_b = proj(OFF_GB, D_GROUP) * (cw[0:1, :] * z2 + cw[1:2, :] * z1 + cw[2:3, :] * z)
    conv_scr[0:CONV_HDR, :] = z[tm - CONV_HDR:tm, :]
    yb_ref[...] = _rms(y_b, mixb_ref[...]).astype(jnp.bfloat16)

    u = proj(OFF_UPOOL, D_GROUP)
    n = tm + POOL_HDR
    p0[POOL_HDR:n, :] = u
    p1[8:n, :] = p0[8:n, :] + p0[7:n - 1, :]
    s2 = p1[POOL_HDR:n, :]
    p2[16:n, :] = p1[16:n, :] + p1[14:n - 2, :]
    s4 = p2[POOL_HDR:n, :]
    p1[24:n, :] = p2[24:n, :] + p2[20:n - 4, :]
    s8 = p1[POOL_HDR:n, :]
    s16 = s8 + p1[24:n - 8, :]
    p0[0:POOL_HDR, :] = u[tm - POOL_HDR:tm, :]
    lane_c = lax.broadcasted_iota(jnp.int32, (tm, D_GROUP), 1)
    row_c = lax.broadcasted_iota(jnp.int32, (tm, D_GROUP), 0)
    win = jnp.where(lane_c < POOL_CH, s2, jnp.where(lane_c < 2 * POOL_CH, s4,
                    jnp.where(lane_c < 3 * POOL_CH, s8, s16)))
    width = jnp.where(lane_c < POOL_CH, POOL_WINDOWS[0], jnp.where(lane_c < 2 * POOL_CH, POOL_WINDOWS[1],
                      jnp.where(lane_c < 3 * POOL_CH, POOL_WINDOWS[2], POOL_WINDOWS[3])))
    count = jnp.minimum(pos0 + row_c + 1, width).astype(jnp.float32)
    pooled = win / count - u
    y_c = _dot(pooled.astype(jnp.bfloat16), w_pool_ref[...]) * pool_scale_ref[...]
    yc_ref[...] = _rms(y_c, mixc_ref[...]).astype(jnp.bfloat16)

    qsw_ref[...] = (proj(OFF_QSW, D_GROUP) * (LOG2E / math.sqrt(SWA_HEAD_DIM))).astype(jnp.bfloat16)
    lane_s = lax.broadcasted_iota(jnp.int32, (tm, SLOT), 1)
    a = proj(OFF_KSW, SLOT)
    r = pltpu.roll(a, SWA_HEAD_DIM, axis=1)
    ksw_ref[...] = jnp.concatenate([jnp.where(lane_s < SWA_HEAD_DIM, a, r),
                                    jnp.where(lane_s < SWA_HEAD_DIM, r, a)], axis=1).astype(jnp.bfloat16)

    lane_v = lax.broadcasted_iota(jnp.int32, (tm, VT_PAD), 1)
    ones_col = jnp.where((lane_v % VT_ROWS == MLA_V) & (lane_v < VT_ALL), 1.0, 0.0)
    v_t = jnp.concatenate([proj(OFF_VSW, SLOT), kv[:, MLA_W:] + ones_col], axis=1).T
    vswt_ref[...] = v_t[:SLOT, :].astype(jnp.bfloat16)
    vt_ref[...] = v_t[SLOT:SLOT + VT_ALL, :].astype(jnp.bfloat16)


def _in_stage(x2d, lw, tabs, seq):
    n_tok = x2d.shape[0]
    tm = TM_IN
    tiles_per_seq = seq // tm
    const = lambda shape: pl.BlockSpec(shape, lambda t: (0, 0), pipeline_mode=pl.Buffered(1))
    tile = lambda w: pl.BlockSpec((tm, w), lambda t: (t, 0))
    vt_spec = lambda rows: pl.BlockSpec((None, rows, tm), lambda t: (t // tiles_per_seq, 0, t % tiles_per_seq))
    bf = jnp.bfloat16
    tok = lambda w: jax.ShapeDtypeStruct((n_tok, w), bf)
    out_shape = [tok(MLA_W), tok(MLA_W), jax.ShapeDtypeStruct((n_tok // seq, VT_ALL, seq), bf),
                 tok(D_GROUP), tok(D_GROUP), tok(D_GROUP), tok(D_GROUP),
                 jax.ShapeDtypeStruct((n_tok // seq, SLOT, seq), bf)]
    return pl.pallas_call(
        functools.partial(_in_stage_kernel, tiles_per_seq=tiles_per_seq),
        grid=(n_tok // tm,),
        in_specs=[tile(D_MODEL), const((1, D_MODEL)), const((D_MODEL, D_IN_PAD)),
                  const((1, MLA_Q_RANK)), const((MLA_Q_RANK, MLA_W)),
                  const((1, MLA_KV_RANK)), const((MLA_KV_RANK, MLA_W + VT_PAD)),
                  const((seq, SLOT)), const((seq, SLOT)),
                  const((CONV_WIDTH, D_GROUP)), const((D_GROUP, D_GROUP)), const((1, D_GROUP)),
                  const((1, D_GROUP)), const((1, D_GROUP))],
        out_specs=[tile(MLA_W), tile(MLA_W), vt_spec(VT_ALL), tile(D_GROUP), tile(D_GROUP),
                   tile(D_GROUP), tile(D_GROUP), vt_spec(SLOT)],
        out_shape=out_shape,
        scratch_shapes=[pltpu.VMEM((tm + CONV_HDR, D_GROUP), jnp.float32),
                        pltpu.VMEM((tm + POOL_HDR, D_GROUP), jnp.float32),
                        pltpu.VMEM((tm + POOL_HDR, D_GROUP), jnp.float32),
                        pltpu.VMEM((tm + POOL_HDR, D_GROUP), jnp.float32)],
        compiler_params=pltpu.CompilerParams(dimension_semantics=("arbitrary",),
                                             vmem_limit_bytes=VMEM_LIMIT_IN_STAGE),
        name="in_stage",
    )(x2d, lw["g_attn"], lw["w_in"], lw["g_q"], lw["w_uq"], lw["g_kv"], lw["w_kv"],
      tabs["tq"], tabs["tk"], lw["conv_w"], lw["w_pool"], lw["pool_scale"], lw["mix_b"], lw["mix_c"])


def _mla_kernel(q_ref, k_ref, vt_ref, mix_ref, o_ref, acc_scr, s_scr):
    tq = q_ref.shape[0]
    i = pl.program_id(1)
    acc_scr[...] = jnp.zeros(acc_scr.shape, jnp.float32)

    def scores(h, cols, key_start):
        k_h = k_ref[pl.ds(key_start, TK), h * SLOT:(h + 1) * SLOT]
        return _dot_nt(k_h, q_ref[cols, h * SLOT:(h + 1) * SLOT])

    def accumulate(h, s, m_old, cols, key_start, mask):
        if mask is not None:
            s = jnp.where(mask, s, NEG_BIG)
        m_new = jnp.maximum(m_old, jnp.max(s, axis=0, keepdims=True))
        p = jnp.exp2(s - m_new).astype(jnp.bfloat16)
        vt_h = vt_ref[h * VT_ROWS:(h + 1) * VT_ROWS, pl.ds(key_start, TK)]
        acc_scr[h, :, cols] = jnp.exp2(m_old - m_new) * acc_scr[h, :, cols] + _dot(vt_h, p)
        return m_new

    def run_units(units, m, following):
        m = list(m)
        s_next = s_scr[...]
        for u, (h, cols, key_start, mask) in enumerate(units):
            s_cur = s_next
            nxt = units[u + 1] if u + 1 < len(units) else following
            if nxt is not None:
                s_next = scores(*nxt[:3])
            ncol = cols.stop - cols.start
            m_new = accumulate(h, s_cur, m[h][:, m[h].shape[1] - ncol:], cols, key_start, mask)
            m[h] = m_new
        if following is not None:
            s_scr[...] = s_next
        return tuple(m)

    all_cols = slice(0, tq)
    tiles_per_step = tq // TK
    tile_start = lambda t: pl.multiple_of(t * TK, TK)

    def full_tiles(j, m):
        units = []
        for t in range(tiles_per_step):
            units += [(h, all_cols, tile_start(j * tiles_per_step + t), None) for h in range(MLA_HEADS)]
        return run_units(units, m, following=(0, all_cols, tile_start((j + 1) * tiles_per_step)))

    s_scr[...] = scores(0, all_cols, 0)
    m = tuple(jnp.full((1, tq), NEG_BIG, jnp.float32) for _ in range(MLA_HEADS))
    m = lax.fori_loop(0, i, full_tiles, m)

    units = []
    for d in range(tiles_per_step):
        key_start = pl.multiple_of(i * tq + d * TK, TK)
        ncol = tq - d * TK
        mask = (lax.broadcasted_iota(jnp.int32, (TK, ncol), 0) <= lax.broadcasted_iota(jnp.int32, (TK, ncol), 1))
        units += [(h, slice(d * TK, tq), key_start, mask) for h in range(MLA_HEADS)]
    run_units(units, m, following=None)

    y_t = jnp.concatenate([acc_scr[h, 0:MLA_V, :] / acc_scr[h, MLA_V:MLA_V + 1, :] for h in range(MLA_HEADS)], axis=0)
    o_ref[...] = _rms(y_t.T, mix_ref[...]).astype(jnp.bfloat16)


def _mla_attention(q, k, vt, mix_a, batch, seq):
    nq = seq // TQ
    return pl.pallas_call(
        _mla_kernel,
        grid=(batch, nq),
        in_specs=[pl.BlockSpec((TQ, MLA_W), lambda b, i: (b * nq + i, 0)),
                  pl.BlockSpec((seq, MLA_W), lambda b, i: (b, 0)),
                  pl.BlockSpec((None, VT_ALL, seq), lambda b, i: (b, 0, 0)),
                  pl.BlockSpec((1, D_GROUP), lambda b, i: (0, 0))],
        out_specs=pl.BlockSpec((TQ, D_GROUP), lambda b, i: (b * nq + i, 0)),
        out_shape=jax.ShapeDtypeStruct((batch * seq, D_GROUP), jnp.bfloat16),
        scratch_shapes=[pltpu.VMEM((MLA_HEADS, VT_ROWS, TQ), jnp.float32),
                        pltpu.VMEM((TK, TQ), jnp.float32)],
        compiler_params=pltpu.CompilerParams(dimension_semantics=("arbitrary", "arbitrary"),
                                             vmem_limit_bytes=VMEM_LIMIT_ATTENTION),
        name="mla_attention",
    )(q, k, vt, mix_a)


def _swa_kernel(sinks_ref, q_ref, k_ref, vt_ref, bias_ref, mix_ref, o_ref, yt_scr):
    tq = q_ref.shape[0]
    blk = SWA_WINDOW
    i = pl.program_id(1)
    lane_q = lax.broadcasted_iota(jnp.int32, (blk, D_GROUP), 1)
    head_of_col = lax.broadcasted_iota(jnp.int32, (1, SWA_HEADS * blk), 1) // blk
    sink_row = jnp.zeros((1, SWA_HEADS * blk), jnp.float32)
    for h in range(SWA_HEADS):
        sink_row = jnp.where(head_of_col == h, sinks_ref[h] * LOG2E, sink_row)

    def key_start(jb):
        return pl.multiple_of(jnp.maximum(i * tq + (jb - 1) * blk, 0), blk)

    def scores(jb):
        k_t = k_ref[pl.ds(key_start(jb), 2 * blk), :]
        q_b = q_ref[jb * blk:(jb + 1) * blk, :]
        q_stack = jnp.concatenate(
            [jnp.where((lane_q >= h * SWA_HEAD_DIM) & (lane_q < (h + 1) * SWA_HEAD_DIM), q_b, 0)
             for h in range(SWA_HEADS)], axis=0)
        return _dot_nt(k_t, q_stack)

    n_blk = tq // blk
    s_next = scores(0)
    for jb in range(n_blk):
        s_cur = s_next
        if jb + 1 < n_blk:
            s_next = scores(jb + 1)
        vt_t = vt_ref[:, pl.ds(key_start(jb), 2 * blk)]
        s = s_cur + bias_ref[jnp.minimum(i * tq + jb * blk, 1)]
        m = jnp.maximum(jnp.max(s, axis=0, keepdims=True), sink_row)
        if jb + 1 < n_blk:
            m = m + _zero_after(s_next[0:1, :])
        p = jnp.exp2(s - m)
        inv = 1.0 / (jnp.sum(p, axis=0, keepdims=True) + jnp.exp2(sink_row - m))
        o = _dot(vt_t, p.astype(jnp.bfloat16)) * inv
        for h in range(SWA_HEADS):
            yt_scr[h * SWA_HEAD_DIM:(h + 1) * SWA_HEAD_DIM, jb * blk:(jb + 1) * blk] = (
                o[(h // 2) * SWA_HEAD_DIM:(h // 2 + 1) * SWA_HEAD_DIM, h * blk:(h + 1) * blk])
    o_ref[...] = _rms(yt_scr[...].T, mix_ref[...]).astype(jnp.bfloat16)


def _swa_bias_tables(slopes):
    blk = SWA_WINDOW
    key = np.arange(2 * blk)[:, None]
    qry = np.arange(blk)[None, :]
    tabs = []
    for off in (0, blk):
        dist = off + qry - key
        valid = (dist >= 0) & (dist < SWA_WINDOW)
        tabs.append(np.concatenate([np.where(valid, -s * LOG2E * dist, NEG_BIG) for s in slopes], axis=1))
    return jnp.asarray(np.stack(tabs), jnp.float32)


def _swa_attention(sinks, q, k, vt, bias, mix_d, batch, seq):
    tq = 2048
    nq = seq // tq
    blk = SWA_WINDOW
    return pl.pallas_call(
        _swa_kernel,
        grid=(batch, nq),
        in_specs=[pl.BlockSpec(memory_space=pltpu.SMEM),
                  pl.BlockSpec((tq, D_GROUP), lambda b, i: (b * nq + i, 0)),
                  pl.BlockSpec((seq, D_GROUP), lambda b, i: (b, 0)),
                  pl.BlockSpec((None, SLOT, seq), lambda b, i: (b, 0, 0)),
                  pl.BlockSpec((2, 2 * blk, SWA_HEADS * blk), lambda b, i: (0, 0, 0)),
                  pl.BlockSpec((1, D_GROUP), lambda b, i: (0, 0))],
        out_specs=pl.BlockSpec((tq, D_GROUP), lambda b, i: (b * nq + i, 0)),
        out_shape=jax.ShapeDtypeStruct((batch * seq, D_GROUP), jnp.bfloat16),
        scratch_shapes=[pltpu.VMEM((D_GROUP, tq), jnp.float32)],
        compiler_params=pltpu.CompilerParams(dimension_semantics=("arbitrary", "arbitrary"),
                                             vmem_limit_bytes=VMEM_LIMIT_ATTENTION),
        name="swa_attention",
    )(sinks, q, k, vt, bias, mix_d)


def _out_stage_kernel(x_ref, ya_ref, yb_ref, yc_ref, yd_ref, w_o_ref, g_ffn_ref, w_gu_ref,
                      w_down_ref, g_final_ref, o_ref, act_scr, *, final):
    x = x_ref[...]
    for g, y_ref in enumerate((ya_ref, yb_ref, yc_ref, yd_ref)):
        x = x + _dot(y_ref[...], w_o_ref[g * D_GROUP:(g + 1) * D_GROUP, :])
    h2 = _rms(x, g_ffn_ref[...]).astype(jnp.bfloat16)

    for c in range(N_FF_CHUNKS):
        cols = slice(c * FF_CHUNK, (c + 1) * FF_CHUNK)
        gate = _dot(h2, w_gu_ref[:, cols])
        up = _dot(h2, w_gu_ref[:, D_FF + c * FF_CHUNK:D_FF + (c + 1) * FF_CHUNK])
        act_scr[:, cols] = (gate * jax.nn.sigmoid(gate) * up).astype(jnp.bfloat16)
    x = x + _dot(act_scr[...], w_down_ref[...])
    if final:
        x = _rms(x, g_final_ref[...])
    o_ref[...] = x


def _out_stage(x2d, ya, yb, yc, yd, lw, stacked, l, g_final, final):
    n_tok = x2d.shape[0]
    tm = TM_OUT
    const = lambda shape: pl.BlockSpec(shape, lambda t: (0,) * len(shape), pipeline_mode=pl.Buffered(1))
    of_layer = lambda shape: pl.BlockSpec((None,) + shape, lambda t: (l,) + (0,) * len(shape),
                                          pipeline_mode=pl.Buffered(1))
    tile = lambda w: pl.BlockSpec((tm, w), lambda t: (t, 0))
    return pl.pallas_call(
        functools.partial(_out_stage_kernel, final=final),
        grid=(n_tok // tm,),
        in_specs=[tile(D_MODEL), tile(D_GROUP), tile(D_GROUP), tile(D_GROUP), tile(D_GROUP),
                  of_layer((D_MODEL, D_MODEL)), const((1, D_MODEL)), of_layer((D_MODEL, 2 * D_FF)),
                  of_layer((D_FF, D_MODEL)), const((1, D_MODEL))],
        out_specs=tile(D_MODEL),
        out_shape=jax.ShapeDtypeStruct((n_tok, D_MODEL), jnp.float32),
        scratch_shapes=[pltpu.VMEM((tm, D_FF), jnp.bfloat16)],
        compiler_params=pltpu.CompilerParams(dimension_semantics=("arbitrary",),
                                             vmem_limit_bytes=VMEM_LIMIT_OUT_STAGE),
        name="out_stage",
    )(x2d, ya, yb, yc, yd, stacked["w_o"], lw["g_ffn"], stacked["w_gate_up"], stacked["w_down"], g_final)


def _rope_tables(seq):
    inv = 1.0 / (ROPE_THETA ** (jnp.arange(0, MLA_ROPE, 2, dtype=jnp.float32) / MLA_ROPE))
    ang = jnp.arange(seq, dtype=jnp.float32)[:, None] * inv[None, :]
    cos, sin = jnp.cos(ang), jnp.sin(ang)
    cos2 = jnp.concatenate([cos, cos], axis=1)
    sin2 = jnp.concatenate([sin, sin], axis=1)
    scale = LOG2E / math.sqrt(MLA_NOPE + MLA_ROPE)
    tq = jnp.concatenate([jnp.full((seq, MLA_NOPE), scale, jnp.float32), cos2 * scale, sin2 * scale], axis=1)
    tk = jnp.concatenate([jnp.zeros((seq, MLA_NOPE), jnp.float32), cos2, sin2], axis=1)
    return {"tq": tq, "tk": tk}


def _swap_halves(w):
    half = w.shape[-1] // 2
    return jnp.concatenate([-w[..., half:], w[..., :half]], axis=-1)


def _layer_weights(l, attn_norm, w_in, mla_q_norm, w_uq, mla_kv_norm, w_ukv, conv_w, pool_w, pool_scale,
                   mix_norm, ffn_norm):
    bf = jnp.bfloat16
    f32 = jnp.float32
    wi = w_in[l]
    pts = np.cumsum((0, 256, 128, 32, 256, 256, 256, 256, 256, 128, 128))
    c_q, c_kv, k_r, g_b, g_c, u_conv, u_pool, q_sw, k_sw, v_sw = [wi[:, pts[j]:pts[j + 1]] for j in range(10)]
    zeros = lambda w: jnp.zeros((D_MODEL, w), f32)
    w_in_r = jnp.concatenate([c_q, c_kv, zeros(MLA_NOPE), k_r, _swap_halves(k_r),
                              g_b, g_c, u_conv, u_pool, q_sw, k_sw, v_sw], axis=1)
    wq = w_uq[l].reshape(MLA_Q_RANK, MLA_HEADS, MLA_NOPE + MLA_ROPE)
    wq_rot = wq[..., MLA_NOPE:]
    w_uq_p = jnp.concatenate([wq, _swap_halves(wq_rot)], axis=-1).reshape(MLA_Q_RANK, MLA_W)
    wkv = w_ukv[l].reshape(MLA_KV_RANK, MLA_HEADS, MLA_NOPE + MLA_V)
    zk = jnp.zeros((MLA_KV_RANK, MLA_HEADS, SLOT - MLA_NOPE), f32)
    w_k = jnp.concatenate([wkv[..., :MLA_NOPE], zk], axis=-1).reshape(MLA_KV_RANK, MLA_W)
    zv = jnp.zeros((MLA_KV_RANK, MLA_HEADS, VT_ROWS - MLA_V), f32)
    w_v = jnp.concatenate([wkv[..., MLA_NOPE:], zv], axis=-1).reshape(MLA_KV_RANK, VT_ALL)
    w_v = jnp.concatenate([w_v, jnp.zeros((MLA_KV_RANK, VT_PAD - VT_ALL), f32)], axis=1)
    w_pool = jax.scipy.linalg.block_diag(*[pool_w[l, g] for g in range(len(POOL_WINDOWS))])
    mix = mix_norm[l].reshape(4, 1, D_GROUP)
    return {
        "g_attn": attn_norm[l][None, :], "w_in": w_in_r.astype(bf),
        "g_q": mla_q_norm[l][None, :], "w_uq": w_uq_p.astype(bf),
        "g_kv": mla_kv_norm[l][None, :], "w_kv": jnp.concatenate([w_k, w_v], axis=1).astype(bf),
        "conv_w": conv_w[l], "w_pool": w_pool.astype(bf), "pool_scale": pool_scale[l][None, :],
        "mix_a": mix[0], "mix_b": mix[1], "mix_c": mix[2], "mix_d": mix[3],
        "g_ffn": ffn_norm[l][None, :],
    }


def kernel(x, attn_norm, w_in, mla_q_norm, w_uq, mla_kv_norm, w_ukv, conv_w, pool_w, pool_scale, swa_sinks,
           mix_norm, w_o, ffn_norm, w_gate_up, w_down, final_norm):
    batch, seq, d_model = x.shape
    depth = w_in.shape[0]
    assert d_model == D_MODEL and w_in.shape[2] == D_IN
    assert seq % TM_IN == 0 and seq % TQ == 0 and TQ % TK == 0 and (batch * seq) % TM_OUT == 0
    slopes = tuple(float(2.0 ** (-8.0 * (h + 1) / SWA_HEADS)) for h in range(SWA_HEADS))
    tabs = _rope_tables(seq)
    swa_bias = _swa_bias_tables(slopes)
    x2d = x.reshape(batch * seq, D_MODEL)
    g_final = final_norm[None, :]
    stacked = {"w_o": w_o.astype(jnp.bfloat16), "w_gate_up": w_gate_up.astype(jnp.bfloat16),
               "w_down": w_down.astype(jnp.bfloat16)}
    for l in range(depth):
        lw = _layer_weights(l, attn_norm, w_in, mla_q_norm, w_uq, mla_kv_norm, w_ukv, conv_w, pool_w,
                            pool_scale, mix_norm, ffn_norm)
        q, k, vt, yb, yc, qsw, ksw, vswt = _in_stage(x2d, lw, tabs, seq)
        ya = _mla_attention(q, k, vt, lw["mix_a"], batch, seq)
        yd = _swa_attention(swa_sinks[l], qsw, ksw, vswt, swa_bias, lw["mix_d"], batch, seq)
        x2d = _out_stage(x2d, ya, yb, yc, yd, lw, stacked, l, g_final, final=(l == depth - 1))
    return x2d.reshape(batch, seq, D_MODEL)
```

```python
import functools
import math

import jax
import jax.numpy as jnp
import numpy as np
from jax import lax
from jax.experimental import pallas as pl
from jax.experimental.pallas import tpu as pltpu

D_MODEL = 1024
D_GROUP = 256
MLA_HEADS = 4
MLA_Q_RANK = 256
MLA_KV_RANK = 128
MLA_NOPE = 64
MLA_ROPE = 32
MLA_V = 64
ROPE_THETA = 10000.0
CONV_WIDTH = 3
POOL_WINDOWS = (2, 4, 8, 16)
POOL_CH = 64
SWA_HEADS = 4
SWA_KV_HEADS = 2
SWA_HEAD_DIM = 64
SWA_WINDOW = 128
D_FF = 2816
RMS_EPS = 1e-6
D_IN = 1952

LANES = 128
MIB = 1024 * 1024
VMEM_LIMIT_IN_STAGE = 40 * MIB
VMEM_LIMIT_ATTENTION = 40 * MIB
VMEM_LIMIT_OUT_STAGE = 52 * MIB

D_IN_PAD = 2048
OFF_CQ, OFF_CKV, OFF_KR, OFF_GB, OFF_GC, OFF_UCONV, OFF_UPOOL = 0, 256, 384, 512, 768, 1024, 1280
OFF_QSW, OFF_KSW, OFF_VSW = 1536, 1792, 1920
SLOT = LANES
MLA_W = MLA_HEADS * SLOT
VT_ROWS = 80
VT_ALL = MLA_HEADS * VT_ROWS
VT_PAD = 384
LOG2E = math.log2(math.e)
POOL_HDR = 32
CONV_HDR = 8
NEG_BIG = -1e30

TM_IN = 1024
TQ = 2048
TK = 512
TM_OUT = 1024
FF_CHUNK = 256
N_FF_CHUNKS = D_FF // FF_CHUNK


def _rms(x, g):
    return x * lax.rsqrt(jnp.mean(x * x, axis=-1, keepdims=True) + RMS_EPS) * g


def _dot(a, b):
    return lax.dot_general(a, b, (((1,), (0,)), ((), ())), preferred_element_type=jnp.float32)


def _zero_after(x):
    bits = pltpu.bitcast(x, jnp.int32)
    return lax.shift_right_logical(lax.shift_right_logical(bits, 16), 16).astype(jnp.float32)


def _dot_nt(a, b):
    return lax.dot_general(a, b, (((1,), (1,)), ((), ())), preferred_element_type=jnp.float32)


def _in_stage_kernel(x_ref, g_attn_ref, w_in_ref, g_q_ref, w_uq_ref, g_kv_ref, w_kv_ref,
                     tq_ref, tk_ref, conv_w_ref, w_pool_ref, pool_scale_ref, mixb_ref, mixc_ref,
                     q_ref, k_ref, vt_ref, yb_ref, yc_ref, qsw_ref, ksw_ref, vswt_ref,
                     conv_scr, p0, p1, p2, *, tiles_per_seq):
    tm = x_ref.shape[0]
    t = pl.program_id(0)
    tile_in_seq = t % tiles_per_seq
    pos0 = pl.multiple_of(tile_in_seq * tm, tm)

    @pl.when(tile_in_seq == 0)
    def _():
        conv_scr[0:CONV_HDR, :] = jnp.zeros((CONV_HDR, D_GROUP), jnp.float32)
        p0[0:POOL_HDR, :] = jnp.zeros((POOL_HDR, D_GROUP), jnp.float32)

    h = _rms(x_ref[...], g_attn_ref[...]).astype(jnp.bfloat16)
    proj_all = _dot(h, w_in_ref[...])
    proj = lambda off, width: proj_all[:, off:off + width]

    lane = lax.broadcasted_iota(jnp.int32, (tm, MLA_W), 1) % SLOT
    qn = _rms(proj(OFF_CQ, MLA_Q_RANK), g_q_ref[...]).astype(jnp.bfloat16)
    qa = _dot(qn, w_uq_ref[...])
    tq_tab = tq_ref[pl.ds(pos0, tm), :]
    qp = qa * jnp.concatenate([tq_tab] * MLA_HEADS, axis=1)
    q_rot = pltpu.roll(qp, MLA_W - MLA_ROPE, axis=1)
    q = jnp.where(lane < MLA_NOPE + MLA_ROPE, qp, 0.0) + jnp.where(
        (lane >= MLA_NOPE) & (lane < MLA_NOPE + MLA_ROPE), q_rot, 0.0)
    q_ref[...] = q.astype(jnp.bfloat16)

    ckn = _rms(proj(OFF_CKV, MLA_KV_RANK), g_kv_ref[...]).astype(jnp.bfloat16)
    kv = _dot(ckn, w_kv_ref[...])
    kr = proj(OFF_KR, SLOT) * tk_ref[pl.ds(pos0, tm), :]
    lane1 = lax.broadcasted_iota(jnp.int32, (tm, SLOT), 1)
    kr = jnp.where((lane1 >= MLA_NOPE) & (lane1 < MLA_NOPE + MLA_ROPE),
                   kr + pltpu.roll(kr, SLOT - MLA_ROPE, axis=1), 0.0)
    k_ref[...] = (kv[:, :MLA_W] + jnp.concatenate([kr] * MLA_HEADS, axis=1)).astype(jnp.bfloat16)

    z = proj(OFF_GC, D_GROUP) * proj(OFF_UCONV, D_GROUP)
    conv_scr[CONV_HDR:CONV_HDR + tm, :] = z
    z1 = conv_scr[CONV_HDR - 1:CONV_HDR - 1 + tm, :]
    z2 = conv_scr[CONV_HDR - 2:CONV_HDR - 2 + tm, :]
    cw = conv_w_ref[...]
    y_b = proj(OFF_GB, D_GROUP) * (cw[0:1, :] * z2 + cw[1:2, :] * z1 + cw[2:3, :] * z)
    conv_scr[0:CONV_HDR, :] = z[tm - CONV_HDR:tm, :]
    yb_ref[...] = _rms(y_b, mixb_ref[...]).astype(jnp.bfloat16)

    u = proj(OFF_UPOOL, D_GROUP)
    n = tm + POOL_HDR
    p0[POOL_HDR:n, :] = u
    p1[8:n, :] = p0[8:n, :] + p0[7:n - 1, :]
    s2 = p1[POOL_HDR:n, :]
    p2[16:n, :] = p1[16:n, :] + p1[14:n - 2, :]
    s4 = p2[POOL_HDR:n, :]
    p1[24:n, :] = p2[24:n, :] + p2[20:n - 4, :]
    s8 = p1[POOL_HDR:n, :]
    s16 = s8 + p1[24:n - 8, :]
    p0[0:POOL_HDR, :] = u[tm - POOL_HDR:tm, :]
    lane_c = lax.broadcasted_iota(jnp.int32, (tm, D_GROUP), 1)
    row_c = lax.broadcasted_iota(jnp.int32, (tm, D_GROUP), 0)
    win = jnp.where(lane_c < POOL_CH, s2, jnp.where(lane_c < 2 * POOL_CH, s4,
                    jnp.where(lane_c < 3 * POOL_CH, s8, s16)))
    width = jnp.where(lane_c < POOL_CH, POOL_WINDOWS[0], jnp.where(lane_c < 2 * POOL_CH, POOL_WINDOWS[1],
                      jnp.where(lane_c < 3 * POOL_CH, POOL_WINDOWS[2], POOL_WINDOWS[3])))
    count = jnp.minimum(pos0 + row_c + 1, width).astype(jnp.float32)
    pooled = win / count - u
    y_c = _dot(pooled.astype(jnp.bfloat16), w_pool_ref[...]) * pool_scale_ref[...]
    yc_ref[...] = _rms(y_c, mixc_ref[...]).astype(jnp.bfloat16)

    qsw_ref[...] = (proj(OFF_QSW, D_GROUP) * (LOG2E / math.sqrt(SWA_HEAD_DIM))).astype(jnp.bfloat16)
    lane_s = lax.broadcasted_iota(jnp.int32, (tm, SLOT), 1)
    a = proj(OFF_KSW, SLOT)
    r = pltpu.roll(a, SWA_HEAD_DIM, axis=1)
    ksw_ref[...] = jnp.concatenate([jnp.where(lane_s < SWA_HEAD_DIM, a, r),
                                    jnp.where(lane_s < SWA_HEAD_DIM, r, a)], axis=1).astype(jnp.bfloat16)

    lane_v = lax.broadcasted_iota(jnp.int32, (tm, VT_PAD), 1)
    ones_col = jnp.where((lane_v % VT_ROWS == MLA_V) & (lane_v < VT_ALL), 1.0, 0.0)
    v_t = jnp.concatenate([proj(OFF_VSW, SLOT), kv[:, MLA_W:] + ones_col], axis=1).T
    vswt_ref[...] = v_t[:SLOT, :].astype(jnp.bfloat16)
    vt_ref[...] = v_t[SLOT:SLOT + VT_ALL, :].astype(jnp.bfloat16)


def _in_stage(x2d, lw, tabs, seq):
    n_tok = x2d.shape[0]
    tm = TM_IN
    tiles_per_seq = seq // tm
    const = lambda shape: pl.BlockSpec(shape, lambda t: (0, 0), pipeline_mode=pl.Buffered(1))
    tile = lambda w: pl.BlockSpec((tm, w), lambda t: (t, 0))
    vt_spec = lambda rows: pl.BlockSpec((None, rows, tm), lambda t: (t // tiles_per_seq, 0, t % tiles_per_seq))
    bf = jnp.bfloat16
    tok = lambda w: jax.ShapeDtypeStruct((n_tok, w), bf)
    out_shape = [tok(MLA_W), tok(MLA_W), jax.ShapeDtypeStruct((n_tok // seq, VT_ALL, seq), bf),
                 tok(D_GROUP), tok(D_GROUP), tok(D_GROUP), tok(D_GROUP),
                 jax.ShapeDtypeStruct((n_tok // seq, SLOT, seq), bf)]
    return pl.pallas_call(
        functools.partial(_in_stage_kernel, tiles_per_seq=tiles_per_seq),
        grid=(n_tok // tm,),
        in_specs=[tile(D_MODEL), const((1, D_MODEL)), const((D_MODEL, D_IN_PAD)),
                  const((1, MLA_Q_RANK)), const((MLA_Q_RANK, MLA_W)),
                  const((1, MLA_KV_RANK)), const((MLA_KV_RANK, MLA_W + VT_PAD)),
                  const((seq, SLOT)), const((seq, SLOT)),
                  const((CONV_WIDTH, D_GROUP)), const((D_GROUP, D_GROUP)), const((1, D_GROUP)),
                  const((1, D_GROUP)), const((1, D_GROUP))],
        out_specs=[tile(MLA_W), tile(MLA_W), vt_spec(VT_ALL), tile(D_GROUP), tile(D_GROUP),
                   tile(D_GROUP), tile(D_GROUP), vt_spec(SLOT)],
        out_shape=out_shape,
        scratch_shapes=[pltpu.VMEM((tm + CONV_HDR, D_GROUP), jnp.float32),
                        pltpu.VMEM((tm + POOL_HDR, D_GROUP), jnp.float32),
                        pltpu.VMEM((tm + POOL_HDR, D_GROUP), jnp.float32),
                        pltpu.VMEM((tm + POOL_HDR, D_GROUP), jnp.float32)],
        compiler_params=pltpu.CompilerParams(dimension_semantics=("arbitrary",),
                                             vmem_limit_bytes=VMEM_LIMIT_IN_STAGE),
        name="in_stage",
    )(x2d, lw["g_attn"], lw["w_in"], lw["g_q"], lw["w_uq"], lw["g_kv"], lw["w_kv"],
      tabs["tq"], tabs["tk"], lw["conv_w"], lw["w_pool"], lw["pool_scale"], lw["mix_b"], lw["mix_c"])


def _mla_kernel(q_ref, k_ref, vt_ref, mix_ref, o_ref, acc_scr, s_scr):
    tq = q_ref.shape[0]
    i = pl.program_id(1)
    acc_scr[...] = jnp.zeros(acc_scr.shape, jnp.float32)

    def scores(h, cols, key_start):
        k_h = k_ref[pl.ds(key_start, TK), h * SLOT:(h + 1) * SLOT]
        return _dot_nt(k_h, q_ref[cols, h * SLOT:(h + 1) * SLOT])

    def accumulate(h, s, m_old, cols, key_start, mask):
        if mask is not None:
            s = jnp.where(mask, s, NEG_BIG)
        m_new = jnp.maximum(m_old, jnp.max(s, axis=0, keepdims=True))
        p = jnp.exp2(s - m_new).astype(jnp.bfloat16)
        vt_h = vt_ref[h * VT_ROWS:(h + 1) * VT_ROWS, pl.ds(key_start, TK)]
        acc_scr[h, :, cols] = jnp.exp2(m_old - m_new) * acc_scr[h, :, cols] + _dot(vt_h, p)
        return m_new

    def run_units(units, m, following):
        m = list(m)
        s_next = s_scr[...]
        for u, (h, cols, key_start, mask) in enumerate(units):
            s_cur = s_next
            nxt = units[u + 1] if u + 1 < len(units) else following
            if nxt is not None:
                s_next = scores(*nxt[:3])
            ncol = cols.stop - cols.start
            m_new = accumulate(h, s_cur, m[h][:, m[h].shape[1] - ncol:], cols, key_start, mask)
            m[h] = m_new
        if following is not None:
            s_scr[...] = s_next
        return tuple(m)

    all_cols = slice(0, tq)
    tiles_per_step = tq // TK
    tile_start = lambda t: pl.multiple_of(t * TK, TK)

    def full_tiles(j, m):
        units = []
        for t in range(tiles_per_step):
            units += [(h, all_cols, tile_start(j * tiles_per_step + t), None) for h in range(MLA_HEADS)]
        return run_units(units, m, following=(0, all_cols, tile_start((j + 1) * tiles_per_step)))

    s_scr[...] = scores(0, all_cols, 0)
    m = tuple(jnp.full((1, tq), NEG_BIG, jnp.float32) for _ in range(MLA_HEADS))
    m = lax.fori_loop(0, i, full_tiles, m)

    units = []
    for d in range(tiles_per_step):
        key_start = pl.multiple_of(i * tq + d * TK, TK)
        ncol = tq - d * TK
        mask = (lax.broadcasted_iota(jnp.int32, (TK, ncol), 0) <= lax.broadcasted_iota(jnp.int32, (TK, ncol), 1))
        units += [(h, slice(d * TK, tq), key_start, mask) for h in range(MLA_HEADS)]
    run_units(units, m, following=None)

    y_t = jnp.concatenate([acc_scr[h, 0:MLA_V, :] / acc_scr[h, MLA_V:MLA_V + 1, :] for h in range(MLA_HEADS)], axis=0)
    o_ref[...] = _rms(y_t.T, mix_ref[...]).astype(jnp.bfloat16)


def _mla_attention(q, k, vt, mix_a, batch, seq):
    nq = seq // TQ
    return pl.pallas_call(
        _mla_kernel,
        grid=(batch, nq),
        in_specs=[pl.BlockSpec((TQ, MLA_W), lambda b, i: (b * nq + i, 0)),
                  pl.BlockSpec((seq, MLA_W), lambda b, i: (b, 0)),
                  pl.BlockSpec((None, VT_ALL, seq), lambda b, i: (b, 0, 0)),
                  pl.BlockSpec((1, D_GROUP), lambda b, i: (0, 0))],
        out_specs=pl.BlockSpec((TQ, D_GROUP), lambda b, i: (b * nq + i, 0)),
        out_shape=jax.ShapeDtypeStruct((batch * seq, D_GROUP), jnp.bfloat16),
        scratch_shapes=[pltpu.VMEM((MLA_HEADS, VT_ROWS, TQ), jnp.float32),
                        pltpu.VMEM((TK, TQ), jnp.float32)],
        compiler_params=pltpu.CompilerParams(dimension_semantics=("arbitrary", "arbitrary"),
                                             vmem_limit_bytes=VMEM_LIMIT_ATTENTION),
        name="mla_attention",
    )(q, k, vt, mix_a)


def _swa_kernel(sinks_ref, q_ref, k_ref, vt_ref, bias_ref, mix_ref, o_ref, yt_scr):
    tq = q_ref.shape[0]
    blk = SWA_WINDOW
    i = pl.program_id(1)
    lane_q = lax.broadcasted_iota(jnp.int32, (blk, D_GROUP), 1)
    head_of_col = lax.broadcasted_iota(jnp.int32, (1, SWA_HEADS * blk), 1) // blk
    sink_row = jnp.zeros((1, SWA_HEADS * blk), jnp.float32)
    for h in range(SWA_HEADS):
        sink_row = jnp.where(head_of_col == h, sinks_ref[h] * LOG2E, sink_row)

    def key_start(jb):
        return pl.multiple_of(jnp.maximum(i * tq + (jb - 1) * blk, 0), blk)

    def scores(jb):
        k_t = k_ref[pl.ds(key_start(jb), 2 * blk), :]
        q_b = q_ref[jb * blk:(jb + 1) * blk, :]
        q_stack = jnp.concatenate(
            [jnp.where((lane_q >= h * SWA_HEAD_DIM) & (lane_q < (h + 1) * SWA_HEAD_DIM), q_b, 0)
             for h in range(SWA_HEADS)], axis=0)
        return _dot_nt(k_t, q_stack)

    n_blk = tq // blk
    s_next = scores(0)
    for jb in range(n_blk):
        s_cur = s_next
        if jb + 1 < n_blk:
            s_next = scores(jb + 1)
        vt_t = vt_ref[:, pl.ds(key_start(jb), 2 * blk)]
        s = s_cur + bias_ref[jnp.minimum(i * tq + jb * blk, 1)]
        m = jnp.maximum(jnp.max(s, axis=0, keepdims=True), sink_row)
        if jb + 1 < n_blk:
            m = m + _zero_after(s_next[0:1, :])
        p = jnp.exp2(s - m)
        inv = 1.0 / (jnp.sum(p, axis=0, keepdims=True) + jnp.exp2(sink_row - m))
        o = _dot(vt_t, p.astype(jnp.bfloat16)) * inv
        for h in range(SWA_HEADS):
            yt_scr[h * SWA_HEAD_DIM:(h + 1) * SWA_HEAD_DIM, jb * blk:(jb + 1) * blk] = (
                o[(h // 2) * SWA_HEAD_DIM:(h // 2 + 1) * SWA_HEAD_DIM, h * blk:(h + 1) * blk])
    o_ref[...] = _rms(yt_scr[...].T, mix_ref[...]).astype(jnp.bfloat16)


def _swa_bias_tables(slopes):
    blk = SWA_WINDOW
    key = np.arange(2 * blk)[:, None]
    qry = np.arange(blk)[None, :]
    tabs = []
    for off in (0, blk):
        dist = off + qry - key
        valid = (dist >= 0) & (dist < SWA_WINDOW)
        tabs.append(np.concatenate([np.where(valid, -s * LOG2E * dist, NEG_BIG) for s in slopes], axis=1))
    return jnp.asarray(np.stack(tabs), jnp.float32)


def _swa_attention(sinks, q, k, vt, bias, mix_d, batch, seq):
    tq = 2048
    nq = seq // tq
    blk = SWA_WINDOW
    return pl.pallas_call(
        _swa_kernel,
        grid=(batch, nq),
        in_specs=[pl.BlockSpec(memory_space=pltpu.SMEM),
                  pl.BlockSpec((tq, D_GROUP), lambda b, i: (b * nq + i, 0)),
                  pl.BlockSpec((seq, D_GROUP), lambda b, i: (b, 0)),
                  pl.BlockSpec((None, SLOT, seq), lambda b, i: (b, 0, 0)),
                  pl.BlockSpec((2, 2 * blk, SWA_HEADS * blk), lambda b, i: (0, 0, 0)),
                  pl.BlockSpec((1, D_GROUP), lambda b, i: (0, 0))],
        out_specs=pl.BlockSpec((tq, D_GROUP), lambda b, i: (b * nq + i, 0)),
        out_shape=jax.ShapeDtypeStruct((batch * seq, D_GROUP), jnp.bfloat16),
        scratch_shapes=[pltpu.VMEM((D_GROUP, tq), jnp.float32)],
        compiler_params=pltpu.CompilerParams(dimension_semantics=("arbitrary", "arbitrary"),
                                             vmem_limit_bytes=VMEM_LIMIT_ATTENTION),
        name="swa_attention",
    )(sinks, q, k, vt, bias, mix_d)


def _out_stage_kernel(x_ref, ya_ref, yb_ref, yc_ref, yd_ref, w_o_ref, g_ffn_ref, w_gu_ref,
                      w_down_ref, g_final_ref, o_ref, act_scr, *, final):
    x = x_ref[...]
    for g, y_ref in enumerate((ya_ref, yb_ref, yc_ref, yd_ref)):
        x = x + _dot(y_ref[...], w_o_ref[g * D_GROUP:(g + 1) * D_GROUP, :])
    h2 = _rms(x, g_ffn_ref[...]).astype(jnp.bfloat16)

    for c in range(N_FF_CHUNKS):
        cols = slice(c * FF_CHUNK, (c + 1) * FF_CHUNK)
        gate = _dot(h2, w_gu_ref[:, cols])
        up = _dot(h2, w_gu_ref[:, D_FF + c * FF_CHUNK:D_FF + (c + 1) * FF_CHUNK])
        act_scr[:, cols] = (gate * jax.nn.sigmoid(gate) * up).astype(jnp.bfloat16)
    x = x + _dot(act_scr[...], w_down_ref[...])
    if final:
        x = _rms(x, g_final_ref[...])
    o_ref[...] = x


def _out_stage(x2d, ya, yb, yc, yd, lw, stacked, l, g_final, final):
    n_tok = x2d.shape[0]
    tm = TM_OUT
    const = lambda shape: pl.BlockSpec(shape, lambda t: (0,) * len(shape), pipeline_mode=pl.Buffered(1))
    of_layer = lambda shape: pl.BlockSpec((None,) + shape, lambda t: (l,) + (0,) * len(shape),
                                          pipeline_mode=pl.Buffered(1))
    tile = lambda w: pl.BlockSpec((tm, w), lambda t: (t, 0))
    return pl.pallas_call(
        functools.partial(_out_stage_kernel, final=final),
        grid=(n_tok // tm,),
        in_specs=[tile(D_MODEL), tile(D_GROUP), tile(D_GROUP), tile(D_GROUP), tile(D_GROUP),
                  of_layer((D_MODEL, D_MODEL)), const((1, D_MODEL)), of_layer((D_MODEL, 2 * D_FF)),
                  of_layer((D_FF, D_MODEL)), const((1, D_MODEL))],
        out_specs=tile(D_MODEL),
        out_shape=jax.ShapeDtypeStruct((n_tok, D_MODEL), jnp.float32),
        scratch_shapes=[pltpu.VMEM((tm, D_FF), jnp.bfloat16)],
        compiler_params=pltpu.CompilerParams(dimension_semantics=("arbitrary",),
                                             vmem_limit_bytes=VMEM_LIMIT_OUT_STAGE),
        name="out_stage",
    )(x2d, ya, yb, yc, yd, stacked["w_o"], lw["g_ffn"], stacked["w_gate_up"], stacked["w_down"], g_final)


def _rope_tables(seq):
    inv = 1.0 / (ROPE_THETA ** (jnp.arange(0, MLA_ROPE, 2, dtype=jnp.float32) / MLA_ROPE))
    ang = jnp.arange(seq, dtype=jnp.float32)[:, None] * inv[None, :]
    cos, sin = jnp.cos(ang), jnp.sin(ang)
    cos2 = jnp.concatenate([cos, cos], axis=1)
    sin2 = jnp.concatenate([sin, sin], axis=1)
    scale = LOG2E / math.sqrt(MLA_NOPE + MLA_ROPE)
    tq = jnp.concatenate([jnp.full((seq, MLA_NOPE), scale, jnp.float32), cos2 * scale, sin2 * scale], axis=1)
    tk = jnp.concatenate([jnp.zeros((seq, MLA_NOPE), jnp.float32), cos2, sin2], axis=1)
    return {"tq": tq, "tk": tk}


def _swap_halves(w):
    half = w.shape[-1] // 2
    return jnp.concatenate([-w[..., half:], w[..., :half]], axis=-1)


def _layer_weights(l, attn_norm, w_in, mla_q_norm, w_uq, mla_kv_norm, w_ukv, conv_w, pool_w, pool_scale,
                   mix_norm, ffn_norm):
    bf = jnp.bfloat16
    f32 = jnp.float32
    wi = w_in[l]
    pts = np.cumsum((0, 256, 128, 32, 256, 256, 256, 256, 256, 128, 128))
    c_q, c_kv, k_r, g_b, g_c, u_conv, u_pool, q_sw, k_sw, v_sw = [wi[:, pts[j]:pts[j + 1]] for j in range(10)]
    zeros = lambda w: jnp.zeros((D_MODEL, w), f32)
    w_in_r = jnp.concatenate([c_q, c_kv, zeros(MLA_NOPE), k_r, _swap_halves(k_r),
                              g_b, g_c, u_conv, u_pool, q_sw, k_sw, v_sw], axis=1)
    wq = w_uq[l].reshape(MLA_Q_RANK, MLA_HEADS, MLA_NOPE + MLA_ROPE)
    wq_rot = wq[..., MLA_NOPE:]
    w_uq_p = jnp.concatenate([wq, _swap_halves(wq_rot)], axis=-1).reshape(MLA_Q_RANK, MLA_W)
    wkv = w_ukv[l].reshape(MLA_KV_RANK, MLA_HEADS, MLA_NOPE + MLA_V)
    zk = jnp.zeros((MLA_KV_RANK, MLA_HEADS, SLOT - MLA_NOPE), f32)
    w_k = jnp.concatenate([wkv[..., :MLA_NOPE], zk], axis=-1).reshape(MLA_KV_RANK, MLA_W)
    zv = jnp.zeros((MLA_KV_RANK, MLA_HEADS, VT_ROWS - MLA_V), f32)
    w_v = jnp.concatenate([wkv[..., MLA_NOPE:], zv], axis=-1).reshape(MLA_KV_RANK, VT_ALL)
    w_v = jnp.concatenate([w_v, jnp.zeros((MLA_KV_RANK, VT_PAD - VT_ALL), f32)], axis=1)
    w_pool = jax.scipy.linalg.block_diag(*[pool_w[l, g] for g in range(len(POOL_WINDOWS))])
    mix = mix_norm[l].reshape(4, 1, D_GROUP)
    return {
        "g_attn": attn_norm[l][None, :], "w_in": w_in_r.astype(bf),
        "g_q": mla_q_norm[l][None, :], "w_uq": w_uq_p.astype(bf),
        "g_kv": mla_kv_norm[l][None, :], "w_kv": jnp.concatenate([w_k, w_v], axis=1).astype(bf),
        "conv_w": conv_w[l], "w_pool": w_pool.astype(bf), "pool_scale": pool_scale[l][None, :],
        "mix_a": mix[0], "mix_b": mix[1], "mix_c": mix[2], "mix_d": mix[3],
        "g_ffn": ffn_norm[l][None, :],
    }


def kernel(x, attn_norm, w_in, mla_q_norm, w_uq, mla_kv_norm, w_ukv, conv_w, pool_w, pool_scale, swa_sinks,
           mix_norm, w_o, ffn_norm, w_gate_up, w_down, final_norm):
    batch, seq, d_model = x.shape
    depth = w_in.shape[0]
    assert d_model == D_MODEL and w_in.shape[2] == D_IN
    assert seq % TM_IN == 0 and seq % TQ == 0 and TQ % TK == 0 and (batch * seq) % TM_OUT == 0
    slopes = tuple(float(2.0 ** (-8.0 * (h + 1) / SWA_HEADS)) for h in range(SWA_HEADS))
    tabs = _rope_tables(seq)
    swa_bias = _swa_bias_tables(slopes)
    x2d = x.reshape(batch * seq, D_MODEL)
    g_final = final_norm[None, :]
    stacked = {"w_o": w_o.astype(jnp.bfloat16), "w_gate_up": w_gate_up.astype(jnp.bfloat16),
               "w_down": w_down.astype(jnp.bfloat16)}
    for l in range(depth):
        lw = _layer_weights(l, attn_norm, w_in, mla_q_norm, w_uq, mla_kv_norm, w_ukv, conv_w, pool_w,
                            pool_scale, mix_norm, ffn_norm)
        q, k, vt, yb, yc, qsw, ksw, vswt = _in_stage(x2d, lw, tabs, seq)
        ya = _mla_attention(q, k, vt, lw["mix_a"], batch, seq)
        yd = _swa_attention(swa_sinks[l], qsw, ksw, vswt, swa_bias, lw["mix_d"], batch, seq)
        x2d = _out_stage(x2d, ya, yb, yc, yd, lw, stacked, l, g_final, final=(l == depth - 1))
    return x2d.reshape(batch, seq, D_MODEL)
```

```python
import functools
import math

import jax
import jax.numpy as jnp
import numpy as np
from jax import lax
from jax.experimental import pallas as pl
from jax.experimental.pallas import tpu as pltpu

D_MODEL = 1024
D_GROUP = 256
MLA_HEADS = 4
MLA_Q_RANK = 256
MLA_KV_RANK = 128
MLA_NOPE = 64
MLA_ROPE = 32
MLA_V = 64
ROPE_THETA = 10000.0
CONV_WIDTH = 3
POOL_WINDOWS = (2, 4, 8, 16)
POOL_CH = 64
SWA_HEADS = 4
SWA_KV_HEADS = 2
SWA_HEAD_DIM = 64
SWA_WINDOW = 128
D_FF = 2816
RMS_EPS = 1e-6
D_IN = 1952

LANES = 128
MIB = 1024 * 1024
VMEM_LIMIT_IN_STAGE = 40 * MIB
VMEM_LIMIT_ATTENTION = 40 * MIB
VMEM_LIMIT_OUT_STAGE = 52 * MIB

D_IN_PAD = 2048
OFF_CQ, OFF_CKV, OFF_KR, OFF_GB, OFF_GC, OFF_UCONV, OFF_UPOOL = 0, 256, 384, 512, 768, 1024, 1280
OFF_QSW, OFF_KSW, OFF_VSW = 1536, 1792, 1920
SLOT = LANES
MLA_W = MLA_HEADS * SLOT
VT_ROWS = 80
VT_ALL = MLA_HEADS * VT_ROWS
VT_PAD = 384
LOG2E = math.log2(math.e)
POOL_HDR = 32
CONV_HDR = 8
NEG_BIG = -1e30

TM_IN = 1024
TQ = 2048
TK = 512
TM_OUT = 1024
FF_CHUNK = 256
N_FF_CHUNKS = D_FF // FF_CHUNK


def _rms(x, g):
    return x * lax.rsqrt(jnp.mean(x * x, axis=-1, keepdims=True) + RMS_EPS) * g


def _dot(a, b):
    return lax.dot_general(a, b, (((1,), (0,)), ((), ())), preferred_element_type=jnp.float32)


def _zero_after(x):
    bits = pltpu.bitcast(x, jnp.int32)
    return lax.shift_right_logical(lax.shift_right_logical(bits, 16), 16).astype(jnp.float32)


def _dot_nt(a, b):
    return lax.dot_general(a, b, (((1,), (1,)), ((), ())), preferred_element_type=jnp.float32)


def _in_stage_kernel(x_ref, g_attn_ref, w_in_ref, g_q_ref, w_uq_ref, g_kv_ref, w_kv_ref,
                     tq_ref, tk_ref, conv_w_ref, w_pool_ref, pool_scale_ref, mixb_ref, mixc_ref,
                     q_ref, k_ref, vt_ref, yb_ref, yc_ref, qsw_ref, ksw_ref, vswt_ref,
                     conv_scr, p0, p1, p2, *, tiles_per_seq):
    tm = x_ref.shape[0]
    t = pl.program_id(0)
    tile_in_seq = t % tiles_per_seq
    pos0 = pl.multiple_of(tile_in_seq * tm, tm)

    @pl.when(tile_in_seq == 0)
    def _():
        conv_scr[0:CONV_HDR, :] = jnp.zeros((CONV_HDR, D_GROUP), jnp.float32)
        p0[0:POOL_HDR, :] = jnp.zeros((POOL_HDR, D_GROUP), jnp.float32)

    h = _rms(x_ref[...], g_attn_ref[...]).astype(jnp.bfloat16)
    proj_all = _dot(h, w_in_ref[...])
    proj = lambda off, width: proj_all[:, off:off + width]

    lane = lax.broadcasted_iota(jnp.int32, (tm, MLA_W), 1) % SLOT
    qn = _rms(proj(OFF_CQ, MLA_Q_RANK), g_q_ref[...]).astype(jnp.bfloat16)
    qa = _dot(qn, w_uq_ref[...])
    tq_tab = tq_ref[pl.ds(pos0, tm), :]
    qp = qa * jnp.concatenate([tq_tab] * MLA_HEADS, axis=1)
    q_rot = pltpu.roll(qp, MLA_W - MLA_ROPE, axis=1)
    q = jnp.where(lane < MLA_NOPE + MLA_ROPE, qp, 0.0) + jnp.where(
        (lane >= MLA_NOPE) & (lane < MLA_NOPE + MLA_ROPE), q_rot, 0.0)
    q_ref[...] = q.astype(jnp.bfloat16)

    ckn = _rms(proj(OFF_CKV, MLA_KV_RANK), g_kv_ref[...]).astype(jnp.bfloat16)
    kv = _dot(ckn, w_kv_ref[...])
    kr = proj(OFF_KR, SLOT) * tk_ref[pl.ds(pos0, tm), :]
    lane1 = lax.broadcasted_iota(jnp.int32, (tm, SLOT), 1)
    kr = jnp.where((lane1 >= MLA_NOPE) & (lane1 < MLA_NOPE + MLA_ROPE),
                   kr + pltpu.roll(kr, SLOT - MLA_ROPE, axis=1), 0.0)
    k_ref[...] = (kv[:, :MLA_W] + jnp.concatenate([kr] * MLA_HEADS, axis=1)).astype(jnp.bfloat16)

    z = proj(OFF_GC, D_GROUP) * proj(OFF_UCONV, D_GROUP)
    conv_scr[CONV_HDR:CONV_HDR + tm, :] = z
    z1 = conv_scr[CONV_HDR - 1:CONV_HDR - 1 + tm, :]
    z2 = conv_scr[CONV_HDR - 2:CONV_HDR - 2 + tm, :]
    cw = conv_w_ref[...]
    y_b = proj(OFF_GB, D_GROUP) * (cw[0:1, :] * z2 + cw[1:2, :] * z1 + cw[2:3, :] * z)
    conv_scr[0:CONV_HDR, :] = z[tm - CONV_HDR:tm, :]
    yb_ref[...] = _rms(y_b, mixb_ref[...]).astype(jnp.bfloat16)

    u = proj(OFF_UPOOL, D_GROUP)
    n = tm + POOL_HDR
    p0[POOL_HDR:n, :] = u
    p1[8:n, :] = p0[8:n, :] + p0[7:n - 1, :]
    s2 = p1[POOL_HDR:n, :]
    p2[16:n, :] = p1[16:n, :] + p1[14:n - 2, :]
    s4 = p2[POOL_HDR:n, :]
    p1[24:n, :] = p2[24:n, :] + p2[20:n - 4, :]
    s8 = p1[POOL_HDR:n, :]
    s16 = s8 + p1[24:n - 8, :]
    p0[0:POOL_HDR, :] = u[tm - POOL_HDR:tm, :]
    lane_c = lax.broadcasted_iota(jnp.int32, (tm, D_GROUP), 1)
    row_c = lax.broadcasted_iota(jnp.int32, (tm, D_GROUP), 0)
    win = jnp.where(lane_c < POOL_CH, s2, jnp.where(lane_c < 2 * POOL_CH, s4,
                    jnp.where(lane_c < 3 * POOL_CH, s8, s16)))
    width = jnp.where(lane_c < POOL_CH, POOL_WINDOWS[0], jnp.where(lane_c < 2 * POOL_CH, POOL_WINDOWS[1],
                      jnp.where(lane_c < 3 * POOL_CH, POOL_WINDOWS[2], POOL_WINDOWS[3])))
    count = jnp.minimum(pos0 + row_c + 1, width).astype(jnp.float32)
    pooled = win / count - u
    y_c = _dot(pooled.astype(jnp.bfloat16), w_pool_ref[...]) * pool_scale_ref[...]
    yc_ref[...] = _rms(y_c, mixc_ref[...]).astype(jnp.bfloat16)

    qsw_ref[...] = (proj(OFF_QSW, D_GROUP) * (LOG2E / math.sqrt(SWA_HEAD_DIM))).astype(jnp.bfloat16)
    lane_s = lax.broadcasted_iota(jnp.int32, (tm, SLOT), 1)
    a = proj(OFF_KSW, SLOT)
    r = pltpu.roll(a, SWA_HEAD_DIM, axis=1)
    ksw_ref[...] = jnp.concatenate([jnp.where(lane_s < SWA_HEAD_DIM, a, r),
                                    jnp.where(lane_s < SWA_HEAD_DIM, r, a)], axis=1).astype(jnp.bfloat16)

    lane_v = lax.broadcasted_iota(jnp.int32, (tm, VT_PAD), 1)
    ones_col = jnp.where((lane_v % VT_ROWS == MLA_V) & (lane_v < VT_ALL), 1.0, 0.0)
    v_t = jnp.concatenate([proj(OFF_VSW, SLOT), kv[:, MLA_W:] + ones_col], axis=1).T
    vswt_ref[...] = v_t[:SLOT, :].astype(jnp.bfloat16)
    vt_ref[...] = v_t[SLOT:SLOT + VT_ALL, :].astype(jnp.bfloat16)


def _in_stage(x2d, lw, tabs, seq):
    n_tok = x2d.shape[0]
    tm = TM_IN
    tiles_per_seq = seq // tm
    const = lambda shape: pl.BlockSpec(shape, lambda t: (0, 0), pipeline_mode=pl.Buffered(1))
    tile = lambda w: pl.BlockSpec((tm, w), lambda t: (t, 0))
    vt_spec = lambda rows: pl.BlockSpec((None, rows, tm), lambda t: (t // tiles_per_seq, 0, t % tiles_per_seq))
    bf = jnp.bfloat16
    tok = lambda w: jax.ShapeDtypeStruct((n_tok, w), bf)
    out_shape = [tok(MLA_W), tok(MLA_W), jax.ShapeDtypeStruct((n_tok // seq, VT_ALL, seq), bf),
                 tok(D_GROUP), tok(D_GROUP), tok(D_GROUP), tok(D_GROUP),
                 jax.ShapeDtypeStruct((n_tok // seq, SLOT, seq), bf)]
    return pl.pallas_call(
        functools.partial(_in_stage_kernel, tiles_per_seq=tiles_per_seq),
        grid=(n_tok // tm,),
        in_specs=[tile(D_MODEL), const((1, D_MODEL)), const((D_MODEL, D_IN_PAD)),
                  const((1, MLA_Q_RANK)), const((MLA_Q_RANK, MLA_W)),
                  const((1, MLA_KV_RANK)), const((MLA_KV_RANK, MLA_W + VT_PAD)),
                  const((seq, SLOT)), const((seq, SLOT)),
                  const((CONV_WIDTH, D_GROUP)), const((D_GROUP, D_GROUP)), const((1, D_GROUP)),
                  const((1, D_GROUP)), const((1, D_GROUP))],
        out_specs=[tile(MLA_W), tile(MLA_W), vt_spec(VT_ALL), tile(D_GROUP), tile(D_GROUP),
                   tile(D_GROUP), tile(D_GROUP), vt_spec(SLOT)],
        out_shape=out_shape,
        scratch_shapes=[pltpu.VMEM((tm + CONV_HDR, D_GROUP), jnp.float32),
                        pltpu.VMEM((tm + POOL_HDR, D_GROUP), jnp.float32),
                        pltpu.VMEM((tm + POOL_HDR, D_GROUP), jnp.float32),
                        pltpu.VMEM((tm + POOL_HDR, D_GROUP), jnp.float32)],
        compiler_params=pltpu.CompilerParams(dimension_semantics=("arbitrary",),
                                             vmem_limit_bytes=VMEM_LIMIT_IN_STAGE),
        name="in_stage",
    )(x2d, lw["g_attn"], lw["w_in"], lw["g_q"], lw["w_uq"], lw["g_kv"], lw["w_kv"],
      tabs["tq"], tabs["tk"], lw["conv_w"], lw["w_pool"], lw["pool_scale"], lw["mix_b"], lw["mix_c"])


def _mla_kernel(q_ref, k_ref, vt_ref, mix_ref, o_ref, acc_scr, s_scr):
    tq = q_ref.shape[0]
    i = pl.program_id(1)
    acc_scr[...] = jnp.zeros(acc_scr.shape, jnp.float32)

    def scores(h, cols, key_start):
        k_h = k_ref[pl.ds(key_start, TK), h * SLOT:(h + 1) * SLOT]
        return _dot_nt(k_h, q_ref[cols, h * SLOT:(h + 1) * SLOT])

    def accumulate(h, s, m_old, cols, key_start, masked):
        if masked:
            head = jnp.where(causal, s[:, :TK], NEG_BIG)
            s = jnp.concatenate([head, s[:, TK:]], axis=1) if s.shape[1] > TK else head
        m_new = jnp.maximum(m_old, jnp.max(s, axis=0, keepdims=True))
        p = jnp.exp2(s - m_new).astype(jnp.bfloat16)
        vt_h = vt_ref[h * VT_ROWS:(h + 1) * VT_ROWS, pl.ds(key_start, TK)]
        acc_scr[h, :, cols] = jnp.exp2(m_old - m_new) * acc_scr[h, :, cols] + _dot(vt_h, p)
        return m_new

    def run_units(units, m, following):
        m = list(m)
        s_next = s_scr[...]
        for u, (h, cols, key_start, masked) in enumerate(units):
            s_cur = s_next
            nxt = units[u + 1] if u + 1 < len(units) else following
            if nxt is not None:
                s_next = scores(*nxt[:3])
            ncol = cols.stop - cols.start
            m_new = accumulate(h, s_cur, m[h][:, m[h].shape[1] - ncol:], cols, key_start, masked)
            m[h] = m_new
        if following is not None:
            s_scr[...] = s_next
        return tuple(m)

    all_cols = slice(0, tq)
    tiles_per_step = tq // TK
    tile_start = lambda t: pl.multiple_of(t * TK, TK)

    def full_tiles(j, m):
        units = []
        for t in range(tiles_per_step):
            units += [(h, all_cols, tile_start(j * tiles_per_step + t), False) for h in range(MLA_HEADS)]
        return run_units(units, m, following=(0, all_cols, tile_start((j + 1) * tiles_per_step)))

    s_scr[...] = scores(0, all_cols, 0)
    m = tuple(jnp.full((1, tq), NEG_BIG, jnp.float32) for _ in range(MLA_HEADS))
    m = lax.fori_loop(0, i, full_tiles, m)

    causal = lax.broadcasted_iota(jnp.int32, (TK, TK), 0) <= lax.broadcasted_iota(jnp.int32, (TK, TK), 1)
    units = []
    for d in range(tiles_per_step):
        key_start = pl.multiple_of(i * tq + d * TK, TK)
        units += [(h, slice(d * TK, tq), key_start, True) for h in range(MLA_HEADS)]
    run_units(units, m, following=None)

    y_t = jnp.concatenate([acc_scr[h, 0:MLA_V, :] / acc_scr[h, MLA_V:MLA_V + 1, :] for h in range(MLA_HEADS)], axis=0)
    o_ref[...] = _rms(y_t.T, mix_ref[...]).astype(jnp.bfloat16)


def _mla_attention(q, k, vt, mix_a, batch, seq):
    nq = seq // TQ
    return pl.pallas_call(
        _mla_kernel,
        grid=(batch, nq),
        in_specs=[pl.BlockSpec((TQ, MLA_W), lambda b, i: (b * nq + i, 0)),
                  pl.BlockSpec((seq, MLA_W), lambda b, i: (b, 0)),
                  pl.BlockSpec((None, VT_ALL, seq), lambda b, i: (b, 0, 0)),
                  pl.BlockSpec((1, D_GROUP), lambda b, i: (0, 0))],
        out_specs=pl.BlockSpec((TQ, D_GROUP), lambda b, i: (b * nq + i, 0)),
        out_shape=jax.ShapeDtypeStruct((batch * seq, D_GROUP), jnp.bfloat16),
        scratch_shapes=[pltpu.VMEM((MLA_HEADS, VT_ROWS, TQ), jnp.float32),
                        pltpu.VMEM((TK, TQ), jnp.float32)],
        compiler_params=pltpu.CompilerParams(dimension_semantics=("arbitrary", "arbitrary"),
                                             vmem_limit_bytes=VMEM_LIMIT_ATTENTION),
        name="mla_attention",
    )(q, k, vt, mix_a)


def _swa_kernel(sinks_ref, q_ref, k_ref, vt_ref, bias_ref, mix_ref, o_ref, yt_scr):
    tq = q_ref.shape[0]
    blk = SWA_WINDOW
    i = pl.program_id(1)
    lane_q = lax.broadcasted_iota(jnp.int32, (blk, D_GROUP), 1)
    head_of_col = lax.broadcasted_iota(jnp.int32, (1, SWA_HEADS * blk), 1) // blk
    sink_row = jnp.zeros((1, SWA_HEADS * blk), jnp.float32)
    for h in range(SWA_HEADS):
        sink_row = jnp.where(head_of_col == h, sinks_ref[h] * LOG2E, sink_row)

    def key_start(jb):
        return pl.multiple_of(jnp.maximum(i * tq + (jb - 1) * blk, 0), blk)

    def scores(jb):
        k_t = k_ref[pl.ds(key_start(jb), 2 * blk), :]
        q_b = q_ref[jb * blk:(jb + 1) * blk, :]
        q_stack = jnp.concatenate(
            [jnp.where((lane_q >= h * SWA_HEAD_DIM) & (lane_q < (h + 1) * SWA_HEAD_DIM), q_b, 0)
             for h in range(SWA_HEADS)], axis=0)
        return _dot_nt(k_t, q_stack)

    n_blk = tq // blk
    s_next = scores(0)
    for jb in range(n_blk):
        s_cur = s_next
        if jb + 1 < n_blk:
            s_next = scores(jb + 1)
        vt_t = vt_ref[:, pl.ds(key_start(jb), 2 * blk)]
        s = s_cur + bias_ref[jnp.minimum(i * tq + jb * blk, 1)]
        m = jnp.maximum(jnp.max(s, axis=0, keepdims=True), sink_row)
        if jb + 1 < n_blk:
            m = m + _zero_after(s_next[0:1, :])
        p = jnp.exp2(s - m)
        inv = 1.0 / (jnp.sum(p, axis=0, keepdims=True) + jnp.exp2(sink_row - m))
        o = _dot(vt_t, p.astype(jnp.bfloat16)) * inv
        for h in range(SWA_HEADS):
            yt_scr[h * SWA_HEAD_DIM:(h + 1) * SWA_HEAD_DIM, jb * blk:(jb + 1) * blk] = (
                o[(h // 2) * SWA_HEAD_DIM:(h // 2 + 1) * SWA_HEAD_DIM, h * blk:(h + 1) * blk])
    o_ref[...] = _rms(yt_scr[...].T, mix_ref[...]).astype(jnp.bfloat16)


def _swa_bias_tables(slopes):
    blk = SWA_WINDOW
    key = np.arange(2 * blk)[:, None]
    qry = np.arange(blk)[None, :]
    tabs = []
    for off in (0, blk):
        dist = off + qry - key
        valid = (dist >= 0) & (dist < SWA_WINDOW)
        tabs.append(np.concatenate([np.where(valid, -s * LOG2E * dist, NEG_BIG) for s in slopes], axis=1))
    return jnp.asarray(np.stack(tabs), jnp.float32)


def _swa_attention(sinks, q, k, vt, bias, mix_d, batch, seq):
    tq = 4096
    nq = seq // tq
    blk = SWA_WINDOW
    return pl.pallas_call(
        _swa_kernel,
        grid=(batch, nq),
        in_specs=[pl.BlockSpec(memory_space=pltpu.SMEM),
                  pl.BlockSpec((tq, D_GROUP), lambda b, i: (b * nq + i, 0)),
                  pl.BlockSpec((seq, D_GROUP), lambda b, i: (b, 0)),
                  pl.BlockSpec((None, SLOT, seq), lambda b, i: (b, 0, 0)),
                  pl.BlockSpec((2, 2 * blk, SWA_HEADS * blk), lambda b, i: (0, 0, 0)),
                  pl.BlockSpec((1, D_GROUP), lambda b, i: (0, 0))],
        out_specs=pl.BlockSpec((tq, D_GROUP), lambda b, i: (b * nq + i, 0)),
        out_shape=jax.ShapeDtypeStruct((batch * seq, D_GROUP), jnp.bfloat16),
        scratch_shapes=[pltpu.VMEM((D_GROUP, tq), jnp.float32)],
        compiler_params=pltpu.CompilerParams(dimension_semantics=("arbitrary", "arbitrary"),
                                             vmem_limit_bytes=VMEM_LIMIT_ATTENTION),
        name="swa_attention",
    )(sinks, q, k, vt, bias, mix_d)


def _out_stage_kernel(x_ref, ya_ref, yb_ref, yc_ref, yd_ref, w_o_ref, g_ffn_ref, w_gu_ref,
                      w_down_ref, g_final_ref, o_ref, act_scr, *, final):
    x = x_ref[...]
    for g, y_ref in enumerate((ya_ref, yb_ref, yc_ref, yd_ref)):
        x = x + _dot(y_ref[...], w_o_ref[g * D_GROUP:(g + 1) * D_GROUP, :])
    h2 = _rms(x, g_ffn_ref[...]).astype(jnp.bfloat16)

    for c in range(N_FF_CHUNKS):
        cols = slice(c * FF_CHUNK, (c + 1) * FF_CHUNK)
        gate = _dot(h2, w_gu_ref[:, cols])
        up = _dot(h2, w_gu_ref[:, D_FF + c * FF_CHUNK:D_FF + (c + 1) * FF_CHUNK])
        act_scr[:, cols] = (gate * jax.nn.sigmoid(gate) * up).astype(jnp.bfloat16)
    x = x + _dot(act_scr[...], w_down_ref[...])
    if final:
        x = _rms(x, g_final_ref[...])
    o_ref[...] = x


def _out_stage(x2d, ya, yb, yc, yd, lw, stacked, l, g_final, final):
    n_tok = x2d.shape[0]
    tm = TM_OUT
    const = lambda shape: pl.BlockSpec(shape, lambda t: (0,) * len(shape), pipeline_mode=pl.Buffered(1))
    of_layer = lambda shape: pl.BlockSpec((None,) + shape, lambda t: (l,) + (0,) * len(shape),
                                          pipeline_mode=pl.Buffered(1))
    tile = lambda w: pl.BlockSpec((tm, w), lambda t: (t, 0))
    return pl.pallas_call(
        functools.partial(_out_stage_kernel, final=final),
        grid=(n_tok // tm,),
        in_specs=[tile(D_MODEL), tile(D_GROUP), tile(D_GROUP), tile(D_GROUP), tile(D_GROUP),
                  of_layer((D_MODEL, D_MODEL)), const((1, D_MODEL)), of_layer((D_MODEL, 2 * D_FF)),
                  of_layer((D_FF, D_MODEL)), const((1, D_MODEL))],
        out_specs=tile(D_MODEL),
        out_shape=jax.ShapeDtypeStruct((n_tok, D_MODEL), jnp.float32),
        scratch_shapes=[pltpu.VMEM((tm, D_FF), jnp.bfloat16)],
        compiler_params=pltpu.CompilerParams(dimension_semantics=("arbitrary",),
                                             vmem_limit_bytes=VMEM_LIMIT_OUT_STAGE),
        name="out_stage",
    )(x2d, ya, yb, yc, yd, stacked["w_o"], lw["g_ffn"], stacked["w_gate_up"], stacked["w_down"], g_final)


def _rope_tables(seq):
    inv = 1.0 / (ROPE_THETA ** (jnp.arange(0, MLA_ROPE, 2, dtype=jnp.float32) / MLA_ROPE))
    ang = jnp.arange(seq, dtype=jnp.float32)[:, None] * inv[None, :]
    cos, sin = jnp.cos(ang), jnp.sin(ang)
    cos2 = jnp.concatenate([cos, cos], axis=1)
    sin2 = jnp.concatenate([sin, sin], axis=1)
    scale = LOG2E / math.sqrt(MLA_NOPE + MLA_ROPE)
    tq = jnp.concatenate([jnp.full((seq, MLA_NOPE), scale, jnp.float32), cos2 * scale, sin2 * scale], axis=1)
    tk = jnp.concatenate([jnp.zeros((seq, MLA_NOPE), jnp.float32), cos2, sin2], axis=1)
    return {"tq": tq, "tk": tk}


def _swap_halves(w):
    half = w.shape[-1] // 2
    return jnp.concatenate([-w[..., half:], w[..., :half]], axis=-1)


def _layer_weights(l, attn_norm, w_in, mla_q_norm, w_uq, mla_kv_norm, w_ukv, conv_w, pool_w, pool_scale,
                   mix_norm, ffn_norm):
    bf = jnp.bfloat16
    f32 = jnp.float32
    wi = w_in[l]
    pts = np.cumsum((0, 256, 128, 32, 256, 256, 256, 256, 256, 128, 128))
    c_q, c_kv, k_r, g_b, g_c, u_conv, u_pool, q_sw, k_sw, v_sw = [wi[:, pts[j]:pts[j + 1]] for j in range(10)]
    zeros = lambda w: jnp.zeros((D_MODEL, w), f32)
    w_in_r = jnp.concatenate([c_q, c_kv, zeros(MLA_NOPE), k_r, _swap_halves(k_r),
                              g_b, g_c, u_conv, u_pool, q_sw, k_sw, v_sw], axis=1)
    wq = w_uq[l].reshape(MLA_Q_RANK, MLA_HEADS, MLA_NOPE + MLA_ROPE)
    wq_rot = wq[..., MLA_NOPE:]
    w_uq_p = jnp.concatenate([wq, _swap_halves(wq_rot)], axis=-1).reshape(MLA_Q_RANK, MLA_W)
    wkv = w_ukv[l].reshape(MLA_KV_RANK, MLA_HEADS, MLA_NOPE + MLA_V)
    zk = jnp.zeros((MLA_KV_RANK, MLA_HEADS, SLOT - MLA_NOPE), f32)
    w_k = jnp.concatenate([wkv[..., :MLA_NOPE], zk], axis=-1).reshape(MLA_KV_RANK, MLA_W)
    zv = jnp.zeros((MLA_KV_RANK, MLA_HEADS, VT_ROWS - MLA_V), f32)
    w_v = jnp.concatenate([wkv[..., MLA_NOPE:], zv], axis=-1).reshape(MLA_KV_RANK, VT_ALL)
    w_v = jnp.concatenate([w_v, jnp.zeros((MLA_KV_RANK, VT_PAD - VT_ALL), f32)], axis=1)
    w_pool = jax.scipy.linalg.block_diag(*[pool_w[l, g] for g in range(len(POOL_WINDOWS))])
    mix = mix_norm[l].reshape(4, 1, D_GROUP)
    return {
        "g_attn": attn_norm[l][None, :], "w_in": w_in_r.astype(bf),
        "g_q": mla_q_norm[l][None, :], "w_uq": w_uq_p.astype(bf),
        "g_kv": mla_kv_norm[l][None, :], "w_kv": jnp.concatenate([w_k, w_v], axis=1).astype(bf),
        "conv_w": conv_w[l], "w_pool": w_pool.astype(bf), "pool_scale": pool_scale[l][None, :],
        "mix_a": mix[0], "mix_b": mix[1], "mix_c": mix[2], "mix_d": mix[3],
        "g_ffn": ffn_norm[l][None, :],
    }


def kernel(x, attn_norm, w_in, mla_q_norm, w_uq, mla_kv_norm, w_ukv, conv_w, pool_w, pool_scale, swa_sinks,
           mix_norm, w_o, ffn_norm, w_gate_up, w_down, final_norm):
    batch, seq, d_model = x.shape
    depth = w_in.shape[0]
    assert d_model == D_MODEL and w_in.shape[2] == D_IN
    assert seq % TM_IN == 0 and seq % TQ == 0 and TQ % TK == 0 and (batch * seq) % TM_OUT == 0
    slopes = tuple(float(2.0 ** (-8.0 * (h + 1) / SWA_HEADS)) for h in range(SWA_HEADS))
    tabs = _rope_tables(seq)
    swa_bias = _swa_bias_tables(slopes)
    x2d = x.reshape(batch * seq, D_MODEL)
    g_final = final_norm[None, :]
    stacked = {"w_o": w_o.astype(jnp.bfloat16), "w_gate_up": w_gate_up.astype(jnp.bfloat16),
               "w_down": w_down.astype(jnp.bfloat16)}
    for l in range(depth):
        lw = _layer_weights(l, attn_norm, w_in, mla_q_norm, w_uq, mla_kv_norm, w_ukv, conv_w, pool_w,
                            pool_scale, mix_norm, ffn_norm)
        q, k, vt, yb, yc, qsw, ksw, vswt = _in_stage(x2d, lw, tabs, seq)
        ya = _mla_attention(q, k, vt, lw["mix_a"], batch, seq)
        yd = _swa_attention(swa_sinks[l], qsw, ksw, vswt, swa_bias, lw["mix_d"], batch, seq)
        x2d = _out_stage(x2d, ya, yb, yc, yd, lw, stacked, l, g_final, final=(l == depth - 1))
    return x2d.reshape(batch, seq, D_MODEL)
```

```python
import functools
import math

import jax
import jax.numpy as jnp
import numpy as np
from jax import lax
from jax.experimental import pallas as pl
from jax.experimental.pallas import tpu as pltpu

D_MODEL = 1024
D_GROUP = 256
MLA_HEADS = 4
MLA_Q_RANK = 256
MLA_KV_RANK = 128
MLA_NOPE = 64
MLA_ROPE = 32
MLA_V = 64
ROPE_THETA = 10000.0
CONV_WIDTH = 3
POOL_WINDOWS = (2, 4, 8, 16)
POOL_CH = 64
SWA_HEADS = 4
SWA_KV_HEADS = 2
SWA_HEAD_DIM = 64
SWA_WINDOW = 128
D_FF = 2816
RMS_EPS = 1e-6
D_IN = 1952

LANES = 128
MIB = 1024 * 1024
VMEM_LIMIT_IN_STAGE = 40 * MIB
VMEM_LIMIT_ATTENTION = 40 * MIB
VMEM_LIMIT_OUT_STAGE = 56 * MIB

D_IN_PAD = 2048
OFF_CQ, OFF_CKV, OFF_KR, OFF_GB, OFF_GC, OFF_UCONV, OFF_UPOOL = 0, 256, 384, 512, 768, 1024, 1280
OFF_QSW, OFF_KSW, OFF_VSW = 1536, 1792, 1920
SLOT = LANES
MLA_W = MLA_HEADS * SLOT
VT_ROWS = 80
VT_ALL = MLA_HEADS * VT_ROWS
VT_PAD = 384
LOG2E = math.log2(math.e)
POOL_HDR = 32
CONV_HDR = 8
NEG_BIG = -1e30

TM_IN = 1024
TQ = 2048
TK = 512
TM_OUT = 1024
FF_CHUNK = 256
N_FF_CHUNKS = D_FF // FF_CHUNK


def _rms(x, g):
    return x * lax.rsqrt(jnp.mean(x * x, axis=-1, keepdims=True) + RMS_EPS) * g


def _dot(a, b):
    return lax.dot_general(a, b, (((1,), (0,)), ((), ())), preferred_element_type=jnp.float32)


def _zero_after(x):
    bits = pltpu.bitcast(x, jnp.int32)
    return lax.shift_right_logical(lax.shift_right_logical(bits, 16), 16).astype(jnp.float32)


def _dot_nt(a, b):
    return lax.dot_general(a, b, (((1,), (1,)), ((), ())), preferred_element_type=jnp.float32)


def _in_stage_kernel(x_ref, g_attn_ref, w_in_ref, g_q_ref, w_uq_ref, g_kv_ref, w_kv_ref,
                     tq_ref, tk_ref, conv_w_ref, w_pool_ref, pool_scale_ref, mixb_ref, mixc_ref,
                     q_ref, k_ref, vt_ref, yb_ref, yc_ref, qsw_ref, ksw_ref, vswt_ref,
                     conv_scr, p0, p1, p2, *, tiles_per_seq):
    tm = x_ref.shape[0]
    t = pl.program_id(0)
    tile_in_seq = t % tiles_per_seq
    pos0 = pl.multiple_of(tile_in_seq * tm, tm)

    @pl.when(tile_in_seq == 0)
    def _():
        conv_scr[0:CONV_HDR, :] = jnp.zeros((CONV_HDR, D_GROUP), jnp.float32)
        p0[0:POOL_HDR, :] = jnp.zeros((POOL_HDR, D_GROUP), jnp.float32)

    h = x_ref[...] if x_ref.dtype == jnp.bfloat16 else _rms(x_ref[...], g_attn_ref[...]).astype(jnp.bfloat16)
    proj_all = _dot(h, w_in_ref[...])
    proj = lambda off, width: proj_all[:, off:off + width]

    lane = lax.broadcasted_iota(jnp.int32, (tm, MLA_W), 1) % SLOT
    qn = _rms(proj(OFF_CQ, MLA_Q_RANK), g_q_ref[...]).astype(jnp.bfloat16)
    qa = _dot(qn, w_uq_ref[...])
    tq_tab = tq_ref[pl.ds(pos0, tm), :]
    qp = qa * jnp.concatenate([tq_tab] * MLA_HEADS, axis=1)
    q_rot = pltpu.roll(qp, MLA_W - MLA_ROPE, axis=1)
    q = jnp.where(lane < MLA_NOPE + MLA_ROPE, qp, 0.0) + jnp.where(
        (lane >= MLA_NOPE) & (lane < MLA_NOPE + MLA_ROPE), q_rot, 0.0)
    q_ref[...] = q.astype(jnp.bfloat16)

    ckn = _rms(proj(OFF_CKV, MLA_KV_RANK), g_kv_ref[...]).astype(jnp.bfloat16)
    kv = _dot(ckn, w_kv_ref[...])
    kr = proj(OFF_KR, SLOT) * tk_ref[pl.ds(pos0, tm), :]
    lane1 = lax.broadcasted_iota(jnp.int32, (tm, SLOT), 1)
    kr = jnp.where((lane1 >= MLA_NOPE) & (lane1 < MLA_NOPE + MLA_ROPE),
                   kr + pltpu.roll(kr, SLOT - MLA_ROPE, axis=1), 0.0)
    k_ref[...] = (kv[:, :MLA_W] + jnp.concatenate([kr] * MLA_HEADS, axis=1)).astype(jnp.bfloat16)

    z = proj(OFF_GC, D_GROUP) * proj(OFF_UCONV, D_GROUP)
    conv_scr[CONV_HDR:CONV_HDR + tm, :] = z
    z1 = conv_scr[CONV_HDR - 1:CONV_HDR - 1 + tm, :]
    z2 = conv_scr[CONV_HDR - 2:CONV_HDR - 2 + tm, :]
    cw = conv_w_ref[...]
    y_b = proj(OFF_GB, D_GROUP) * (cw[0:1, :] * z2 + cw[1:2, :] * z1 + cw[2:3, :] * z)
    conv_scr[0:CONV_HDR, :] = z[tm - CONV_HDR:tm, :]
    yb_ref[...] = _rms(y_b, mixb_ref[...]).astype(jnp.bfloat16)

    u = proj(OFF_UPOOL, D_GROUP)
    n = tm + POOL_HDR
    p0[POOL_HDR:n, :] = u
    p1[8:n, :] = p0[8:n, :] + p0[7:n - 1, :]
    s2 = p1[POOL_HDR:n, :]
    p2[16:n, :] = p1[16:n, :] + p1[14:n - 2, :]
    s4 = p2[POOL_HDR:n, :]
    p1[24:n, :] = p2[24:n, :] + p2[20:n - 4, :]
    s8 = p1[POOL_HDR:n, :]
    s16 = s8 + p1[24:n - 8, :]
    p0[0:POOL_HDR, :] = u[tm - POOL_HDR:tm, :]
    lane_c = lax.broadcasted_iota(jnp.int32, (tm, D_GROUP), 1)
    row_c = lax.broadcasted_iota(jnp.int32, (tm, D_GROUP), 0)
    win = jnp.where(lane_c < POOL_CH, s2, jnp.where(lane_c < 2 * POOL_CH, s4,
                    jnp.where(lane_c < 3 * POOL_CH, s8, s16)))
    width = jnp.where(lane_c < POOL_CH, POOL_WINDOWS[0], jnp.where(lane_c < 2 * POOL_CH, POOL_WINDOWS[1],
                      jnp.where(lane_c < 3 * POOL_CH, POOL_WINDOWS[2], POOL_WINDOWS[3])))
    count = jnp.minimum(pos0 + row_c + 1, width).astype(jnp.float32)
    pooled = win / count - u
    y_c = _dot(pooled.astype(jnp.bfloat16), w_pool_ref[...]) * pool_scale_ref[...]
    yc_ref[...] = _rms(y_c, mixc_ref[...]).astype(jnp.bfloat16)

    qsw_ref[...] = (proj(OFF_QSW, D_GROUP) * (LOG2E / math.sqrt(SWA_HEAD_DIM))).astype(jnp.bfloat16)
    lane_s = lax.broadcasted_iota(jnp.int32, (tm, SLOT), 1)
    a = proj(OFF_KSW, SLOT)
    r = pltpu.roll(a, SWA_HEAD_DIM, axis=1)
    ksw_ref[...] = jnp.concatenate([jnp.where(lane_s < SWA_HEAD_DIM, a, r),
                                    jnp.where(lane_s < SWA_HEAD_DIM, r, a)], axis=1).astype(jnp.bfloat16)

    lane_v = lax.broadcasted_iota(jnp.int32, (tm, VT_PAD), 1)
    ones_col = jnp.where((lane_v % VT_ROWS == MLA_V) & (lane_v < VT_ALL), 1.0, 0.0)
    v_t = jnp.concatenate([proj(OFF_VSW, SLOT), kv[:, MLA_W:] + ones_col], axis=1).T
    vswt_ref[...] = v_t[:SLOT, :].astype(jnp.bfloat16)
    vt_ref[...] = v_t[SLOT:SLOT + VT_ALL, :].astype(jnp.bfloat16)


def _in_stage(x2d, lw, tabs, seq):
    n_tok = x2d.shape[0]
    tm = TM_IN
    tiles_per_seq = seq // tm
    const = lambda shape: pl.BlockSpec(shape, lambda t: (0, 0), pipeline_mode=pl.Buffered(1))
    tile = lambda w: pl.BlockSpec((tm, w), lambda t: (t, 0))
    vt_spec = lambda rows: pl.BlockSpec((None, rows, tm), lambda t: (t // tiles_per_seq, 0, t % tiles_per_seq))
    bf = jnp.bfloat16
    tok = lambda w: jax.ShapeDtypeStruct((n_tok, w), bf)
    out_shape = [tok(MLA_W), tok(MLA_W), jax.ShapeDtypeStruct((n_tok // seq, VT_ALL, seq), bf),
                 tok(D_GROUP), tok(D_GROUP), tok(D_GROUP), tok(D_GROUP),
                 jax.ShapeDtypeStruct((n_tok // seq, SLOT, seq), bf)]
    return pl.pallas_call(
        functools.partial(_in_stage_kernel, tiles_per_seq=tiles_per_seq),
        grid=(n_tok // tm,),
        in_specs=[tile(D_MODEL), const((1, D_MODEL)), const((D_MODEL, D_IN_PAD)),
                  const((1, MLA_Q_RANK)), const((MLA_Q_RANK, MLA_W)),
                  const((1, MLA_KV_RANK)), const((MLA_KV_RANK, MLA_W + VT_PAD)),
                  const((seq, SLOT)), const((seq, SLOT)),
                  const((CONV_WIDTH, D_GROUP)), const((D_GROUP, D_GROUP)), const((1, D_GROUP)),
                  const((1, D_GROUP)), const((1, D_GROUP))],
        out_specs=[tile(MLA_W), tile(MLA_W), vt_spec(VT_ALL), tile(D_GROUP), tile(D_GROUP),
                   tile(D_GROUP), tile(D_GROUP), vt_spec(SLOT)],
        out_shape=out_shape,
        scratch_shapes=[pltpu.VMEM((tm + CONV_HDR, D_GROUP), jnp.float32),
                        pltpu.VMEM((tm + POOL_HDR, D_GROUP), jnp.float32),
                        pltpu.VMEM((tm + POOL_HDR, D_GROUP), jnp.float32),
                        pltpu.VMEM((tm + POOL_HDR, D_GROUP), jnp.float32)],
        compiler_params=pltpu.CompilerParams(dimension_semantics=("arbitrary",),
                                             vmem_limit_bytes=VMEM_LIMIT_IN_STAGE),
        name="in_stage",
    )(x2d, lw["g_attn"], lw["w_in"], lw["g_q"], lw["w_uq"], lw["g_kv"], lw["w_kv"],
      tabs["tq"], tabs["tk"], lw["conv_w"], lw["w_pool"], lw["pool_scale"], lw["mix_b"], lw["mix_c"])


def _mla_kernel(q_ref, k_ref, vt_ref, mix_ref, o_ref, acc_scr, s_scr):
    tq = q_ref.shape[0]
    i = pl.program_id(1)
    acc_scr[...] = jnp.zeros(acc_scr.shape, jnp.float32)

    def scores(h, cols, key_start):
        k_h = k_ref[pl.ds(key_start, TK), h * SLOT:(h + 1) * SLOT]
        return _dot_nt(k_h, q_ref[cols, h * SLOT:(h + 1) * SLOT])

    def accumulate(h, s, m_old, cols, key_start, masked):
        if masked:
            head = jnp.where(causal, s[:, :TK], NEG_BIG)
            s = jnp.concatenate([head, s[:, TK:]], axis=1) if s.shape[1] > TK else head
        m_new = jnp.maximum(m_old, jnp.max(s, axis=0, keepdims=True))
        p = jnp.exp2(s - m_new).astype(jnp.bfloat16)
        vt_h = vt_ref[h * VT_ROWS:(h + 1) * VT_ROWS, pl.ds(key_start, TK)]
        acc_scr[h, :, cols] = jnp.exp2(m_old - m_new) * acc_scr[h, :, cols] + _dot(vt_h, p)
        return m_new

    def run_units(units, m, following):
        m = list(m)
        s_next = s_scr[...]
        for u, (h, cols, key_start, masked) in enumerate(units):
            s_cur = s_next
            nxt = units[u + 1] if u + 1 < len(units) else following
            if nxt is not None:
                s_next = scores(*nxt[:3])
            ncol = cols.stop - cols.start
            m_new = accumulate(h, s_cur, m[h][:, m[h].shape[1] - ncol:], cols, key_start, masked)
            m[h] = m_new
        if following is not None:
            s_scr[...] = s_next
        return tuple(m)

    all_cols = slice(0, tq)
    tiles_per_step = tq // TK
    tile_start = lambda t: pl.multiple_of(t * TK, TK)

    def full_tiles(j, m):
        units = []
        for t in range(tiles_per_step):
            units += [(h, all_cols, tile_start(j * tiles_per_step + t), False) for h in range(MLA_HEADS)]
        return run_units(units, m, following=(0, all_cols, tile_start((j + 1) * tiles_per_step)))

    s_scr[...] = scores(0, all_cols, 0)
    m = tuple(jnp.full((1, tq), NEG_BIG, jnp.float32) for _ in range(MLA_HEADS))
    m = lax.fori_loop(0, i, full_tiles, m)

    causal = lax.broadcasted_iota(jnp.int32, (TK, TK), 0) <= lax.broadcasted_iota(jnp.int32, (TK, TK), 1)
    units = []
    for d in range(tiles_per_step):
        key_start = pl.multiple_of(i * tq + d * TK, TK)
        units += [(h, slice(d * TK, tq), key_start, True) for h in range(MLA_HEADS)]
    run_units(units, m, following=None)

    y_t = jnp.concatenate([acc_scr[h, 0:MLA_V, :] / acc_scr[h, MLA_V:MLA_V + 1, :] for h in range(MLA_HEADS)], axis=0)
    o_ref[...] = _rms(y_t.T, mix_ref[...]).astype(jnp.bfloat16)


def _mla_attention(q, k, vt, mix_a, batch, seq):
    nq = seq // TQ
    return pl.pallas_call(
        _mla_kernel,
        grid=(batch, nq),
        in_specs=[pl.BlockSpec((TQ, MLA_W), lambda b, i: (b * nq + i, 0)),
                  pl.BlockSpec((seq, MLA_W), lambda b, i: (b, 0)),
                  pl.BlockSpec((None, VT_ALL, seq), lambda b, i: (b, 0, 0)),
                  pl.BlockSpec((1, D_GROUP), lambda b, i: (0, 0))],
        out_specs=pl.BlockSpec((TQ, D_GROUP), lambda b, i: (b * nq + i, 0)),
        out_shape=jax.ShapeDtypeStruct((batch * seq, D_GROUP), jnp.bfloat16),
        scratch_shapes=[pltpu.VMEM((MLA_HEADS, VT_ROWS, TQ), jnp.float32),
                        pltpu.VMEM((TK, TQ), jnp.float32)],
        compiler_params=pltpu.CompilerParams(dimension_semantics=("arbitrary", "arbitrary"),
                                             vmem_limit_bytes=VMEM_LIMIT_ATTENTION),
        name="mla_attention",
    )(q, k, vt, mix_a)


def _swa_kernel(sinks_ref, q_ref, k_ref, vt_ref, bias_ref, mix_ref, o_ref, yt_scr):
    tq = q_ref.shape[0]
    blk = SWA_WINDOW
    i = pl.program_id(1)
    lane_q = lax.broadcasted_iota(jnp.int32, (blk, D_GROUP), 1)
    head_of_col = lax.broadcasted_iota(jnp.int32, (1, SWA_HEADS * blk), 1) // blk
    sink_row = jnp.zeros((1, SWA_HEADS * blk), jnp.float32)
    for h in range(SWA_HEADS):
        sink_row = jnp.where(head_of_col == h, sinks_ref[h] * LOG2E, sink_row)

    def key_start(jb):
        return pl.multiple_of(jnp.maximum(i * tq + (jb - 1) * blk, 0), blk)

    def scores(jb):
        k_t = k_ref[pl.ds(key_start(jb), 2 * blk), :]
        q_b = q_ref[jb * blk:(jb + 1) * blk, :]
        q_stack = jnp.concatenate(
            [jnp.where((lane_q >= h * SWA_HEAD_DIM) & (lane_q < (h + 1) * SWA_HEAD_DIM), q_b, 0)
             for h in range(SWA_HEADS)], axis=0)
        return _dot_nt(k_t, q_stack)

    n_blk = tq // blk
    s_next = scores(0)
    for jb in range(n_blk):
        s_cur = s_next
        if jb + 1 < n_blk:
            s_next = scores(jb + 1)
        vt_t = vt_ref[:, pl.ds(key_start(jb), 2 * blk)]
        s = s_cur + bias_ref[jnp.minimum(i * tq + jb * blk, 1)]
        m = jnp.maximum(jnp.max(s, axis=0, keepdims=True), sink_row)
        if jb + 1 < n_blk:
            m = m + _zero_after(s_next[0:1, :])
        p = jnp.exp2(s - m)
        inv = 1.0 / (jnp.sum(p, axis=0, keepdims=True) + jnp.exp2(sink_row - m))
        o = _dot(vt_t, p.astype(jnp.bfloat16)) * inv
        for h in range(SWA_HEADS):
            yt_scr[h * SWA_HEAD_DIM:(h + 1) * SWA_HEAD_DIM, jb * blk:(jb + 1) * blk] = (
                o[(h // 2) * SWA_HEAD_DIM:(h // 2 + 1) * SWA_HEAD_DIM, h * blk:(h + 1) * blk])
    o_ref[...] = _rms(yt_scr[...].T, mix_ref[...]).astype(jnp.bfloat16)


def _swa_bias_tables(slopes):
    blk = SWA_WINDOW
    key = np.arange(2 * blk)[:, None]
    qry = np.arange(blk)[None, :]
    tabs = []
    for off in (0, blk):
        dist = off + qry - key
        valid = (dist >= 0) & (dist < SWA_WINDOW)
        tabs.append(np.concatenate([np.where(valid, -s * LOG2E * dist, NEG_BIG) for s in slopes], axis=1))
    return jnp.asarray(np.stack(tabs), jnp.float32)


def _swa_attention(sinks, q, k, vt, bias, mix_d, batch, seq):
    tq = 4096
    nq = seq // tq
    blk = SWA_WINDOW
    return pl.pallas_call(
        _swa_kernel,
        grid=(batch, nq),
        in_specs=[pl.BlockSpec(memory_space=pltpu.SMEM),
                  pl.BlockSpec((tq, D_GROUP), lambda b, i: (b * nq + i, 0)),
                  pl.BlockSpec((seq, D_GROUP), lambda b, i: (b, 0)),
                  pl.BlockSpec((None, SLOT, seq), lambda b, i: (b, 0, 0)),
                  pl.BlockSpec((2, 2 * blk, SWA_HEADS * blk), lambda b, i: (0, 0, 0)),
                  pl.BlockSpec((1, D_GROUP), lambda b, i: (0, 0))],
        out_specs=pl.BlockSpec((tq, D_GROUP), lambda b, i: (b * nq + i, 0)),
        out_shape=jax.ShapeDtypeStruct((batch * seq, D_GROUP), jnp.bfloat16),
        scratch_shapes=[pltpu.VMEM((D_GROUP, tq), jnp.float32)],
        compiler_params=pltpu.CompilerParams(dimension_semantics=("arbitrary", "arbitrary"),
                                             vmem_limit_bytes=VMEM_LIMIT_ATTENTION),
        name="swa_attention",
    )(sinks, q, k, vt, bias, mix_d)


def _out_stage_kernel(x_ref, ya_ref, yb_ref, yc_ref, yd_ref, w_o_ref, g_ffn_ref, w_gu_ref,
                      w_down_ref, g_next_ref, *out_and_scratch, final):
    *out_refs, act_scr = out_and_scratch
    x = x_ref[...]
    for g, y_ref in enumerate((ya_ref, yb_ref, yc_ref, yd_ref)):
        x = x + _dot(y_ref[...], w_o_ref[g * D_GROUP:(g + 1) * D_GROUP, :])
    h2 = _rms(x, g_ffn_ref[...]).astype(jnp.bfloat16)

    for c in range(N_FF_CHUNKS):
        cols = slice(c * FF_CHUNK, (c + 1) * FF_CHUNK)
        gate = _dot(h2, w_gu_ref[:, cols])
        up = _dot(h2, w_gu_ref[:, D_FF + c * FF_CHUNK:D_FF + (c + 1) * FF_CHUNK])
        act_scr[:, cols] = (gate * jax.nn.sigmoid(gate) * up).astype(jnp.bfloat16)
    x = x + _dot(act_scr[...], w_down_ref[...])
    normed = _rms(x, g_next_ref[...])
    if final:
        out_refs[0][...] = normed
    else:
        out_refs[0][...] = x
        out_refs[1][...] = normed.astype(jnp.bfloat16)


def _out_stage(x2d, ya, yb, yc, yd, lw, stacked, l, g_next, final):
    n_tok = x2d.shape[0]
    tm = TM_OUT
    f32_out = jax.ShapeDtypeStruct((n_tok, D_MODEL), jnp.float32)
    bf16_out = jax.ShapeDtypeStruct((n_tok, D_MODEL), jnp.bfloat16)
    const = lambda shape: pl.BlockSpec(shape, lambda t: (0,) * len(shape), pipeline_mode=pl.Buffered(1))
    of_layer = lambda shape: pl.BlockSpec((None,) + shape, lambda t: (l,) + (0,) * len(shape),
                                          pipeline_mode=pl.Buffered(1))
    tile = lambda w: pl.BlockSpec((tm, w), lambda t: (t, 0))
    return pl.pallas_call(
        functools.partial(_out_stage_kernel, final=final),
        grid=(n_tok // tm,),
        in_specs=[tile(D_MODEL), tile(D_GROUP), tile(D_GROUP), tile(D_GROUP), tile(D_GROUP),
                  of_layer((D_MODEL, D_MODEL)), const((1, D_MODEL)), of_layer((D_MODEL, 2 * D_FF)),
                  of_layer((D_FF, D_MODEL)), const((1, D_MODEL))],
        out_specs=[tile(D_MODEL)] if final else [tile(D_MODEL), tile(D_MODEL)],
        out_shape=[f32_out] if final else [f32_out, bf16_out],
        scratch_shapes=[pltpu.VMEM((tm, D_FF), jnp.bfloat16)],
        compiler_params=pltpu.CompilerParams(dimension_semantics=("arbitrary",),
                                             vmem_limit_bytes=VMEM_LIMIT_OUT_STAGE),
        name="out_stage",
    )(x2d, ya, yb, yc, yd, stacked["w_o"], lw["g_ffn"], stacked["w_gate_up"], stacked["w_down"], g_next)


def _rope_tables(seq):
    inv = 1.0 / (ROPE_THETA ** (jnp.arange(0, MLA_ROPE, 2, dtype=jnp.float32) / MLA_ROPE))
    ang = jnp.arange(seq, dtype=jnp.float32)[:, None] * inv[None, :]
    cos, sin = jnp.cos(ang), jnp.sin(ang)
    cos2 = jnp.concatenate([cos, cos], axis=1)
    sin2 = jnp.concatenate([sin, sin], axis=1)
    scale = LOG2E / math.sqrt(MLA_NOPE + MLA_ROPE)
    tq = jnp.concatenate([jnp.full((seq, MLA_NOPE), scale, jnp.float32), cos2 * scale, sin2 * scale], axis=1)
    tk = jnp.concatenate([jnp.zeros((seq, MLA_NOPE), jnp.float32), cos2, sin2], axis=1)
    return {"tq": tq, "tk": tk}


def _swap_halves(w):
    half = w.shape[-1] // 2
    return jnp.concatenate([-w[..., half:], w[..., :half]], axis=-1)


def _layer_weights(l, attn_norm, w_in, mla_q_norm, w_uq, mla_kv_norm, w_ukv, conv_w, pool_w, pool_scale,
                   mix_norm, ffn_norm):
    bf = jnp.bfloat16
    f32 = jnp.float32
    wi = w_in[l]
    pts = np.cumsum((0, 256, 128, 32, 256, 256, 256, 256, 256, 128, 128))
    c_q, c_kv, k_r, g_b, g_c, u_conv, u_pool, q_sw, k_sw, v_sw = [wi[:, pts[j]:pts[j + 1]] for j in range(10)]
    zeros = lambda w: jnp.zeros((D_MODEL, w), f32)
    w_in_r = jnp.concatenate([c_q, c_kv, zeros(MLA_NOPE), k_r, _swap_halves(k_r),
                              g_b, g_c, u_conv, u_pool, q_sw, k_sw, v_sw], axis=1)
    wq = w_uq[l].reshape(MLA_Q_RANK, MLA_HEADS, MLA_NOPE + MLA_ROPE)
    wq_rot = wq[..., MLA_NOPE:]
    w_uq_p = jnp.concatenate([wq, _swap_halves(wq_rot)], axis=-1).reshape(MLA_Q_RANK, MLA_W)
    wkv = w_ukv[l].reshape(MLA_KV_RANK, MLA_HEADS, MLA_NOPE + MLA_V)
    zk = jnp.zeros((MLA_KV_RANK, MLA_HEADS, SLOT - MLA_NOPE), f32)
    w_k = jnp.concatenate([wkv[..., :MLA_NOPE], zk], axis=-1).reshape(MLA_KV_RANK, MLA_W)
    zv = jnp.zeros((MLA_KV_RANK, MLA_HEADS, VT_ROWS - MLA_V), f32)
    w_v = jnp.concatenate([wkv[..., MLA_NOPE:], zv], axis=-1).reshape(MLA_KV_RANK, VT_ALL)
    w_v = jnp.concatenate([w_v, jnp.zeros((MLA_KV_RANK, VT_PAD - VT_ALL), f32)], axis=1)
    w_pool = jax.scipy.linalg.block_diag(*[pool_w[l, g] for g in range(len(POOL_WINDOWS))])
    mix = mix_norm[l].reshape(4, 1, D_GROUP)
    return {
        "g_attn": attn_norm[l][None, :], "w_in": w_in_r.astype(bf),
        "g_q": mla_q_norm[l][None, :], "w_uq": w_uq_p.astype(bf),
        "g_kv": mla_kv_norm[l][None, :], "w_kv": jnp.concatenate([w_k, w_v], axis=1).astype(bf),
        "conv_w": conv_w[l], "w_pool": w_pool.astype(bf), "pool_scale": pool_scale[l][None, :],
        "mix_a": mix[0], "mix_b": mix[1], "mix_c": mix[2], "mix_d": mix[3],
        "g_ffn": ffn_norm[l][None, :],
    }


def kernel(x, attn_norm, w_in, mla_q_norm, w_uq, mla_kv_norm, w_ukv, conv_w, pool_w, pool_scale, swa_sinks,
           mix_norm, w_o, ffn_norm, w_gate_up, w_down, final_norm):
    batch, seq, d_model = x.shape
    depth = w_in.shape[0]
    assert d_model == D_MODEL and w_in.shape[2] == D_IN
    assert seq % TM_IN == 0 and seq % TQ == 0 and TQ % TK == 0 and (batch * seq) % TM_OUT == 0
    slopes = tuple(float(2.0 ** (-8.0 * (h + 1) / SWA_HEADS)) for h in range(SWA_HEADS))
    tabs = _rope_tables(seq)
    swa_bias = _swa_bias_tables(slopes)
    x2d = x.reshape(batch * seq, D_MODEL)
    stream = x2d
    stacked = {"w_o": w_o.astype(jnp.bfloat16), "w_gate_up": w_gate_up.astype(jnp.bfloat16),
               "w_down": w_down.astype(jnp.bfloat16)}
    for l in range(depth):
        lw = _layer_weights(l, attn_norm, w_in, mla_q_norm, w_uq, mla_kv_norm, w_ukv, conv_w, pool_w,
                            pool_scale, mix_norm, ffn_norm)
        q, k, vt, yb, yc, qsw, ksw, vswt = _in_stage(stream, lw, tabs, seq)
        ya = _mla_attention(q, k, vt, lw["mix_a"], batch, seq)
        yd = _swa_attention(swa_sinks[l], qsw, ksw, vswt, swa_bias, lw["mix_d"], batch, seq)
        final = l == depth - 1
        g_next = (final_norm if final else attn_norm[l + 1])[None, :]
        outs = _out_stage(x2d, ya, yb, yc, yd, lw, stacked, l, g_next, final=final)
        if final:
            return outs[0].reshape(batch, seq, D_MODEL)
        x2d, stream = outs
```

```python
import functools
import math

import jax
import jax.numpy as jnp
import numpy as np
from jax import lax
from jax.experimental import pallas as pl
from jax.experimental.pallas import tpu as pltpu

D_MODEL = 1024
D_GROUP = 256
MLA_HEADS = 4
MLA_Q_RANK = 256
MLA_KV_RANK = 128
MLA_NOPE = 64
MLA_ROPE = 32
MLA_V = 64
ROPE_THETA = 10000.0
CONV_WIDTH = 3
POOL_WINDOWS = (2, 4, 8, 16)
POOL_CH = 64
SWA_HEADS = 4
SWA_KV_HEADS = 2
SWA_HEAD_DIM = 64
SWA_WINDOW = 128
D_FF = 2816
RMS_EPS = 1e-6
D_IN = 1952

LANES = 128
MIB = 1024 * 1024
VMEM_LIMIT_IN_STAGE = 40 * MIB
VMEM_LIMIT_ATTENTION = 40 * MIB
VMEM_LIMIT_OUT_STAGE = 52 * MIB

D_IN_PAD = 2048
OFF_CQ, OFF_CKV, OFF_KR, OFF_GB, OFF_GC, OFF_UCONV, OFF_UPOOL = 0, 256, 384, 512, 768, 1024, 1280
OFF_QSW, OFF_KSW, OFF_VSW = 1536, 1792, 1920
SLOT = LANES
MLA_W = MLA_HEADS * SLOT
VT_ROWS = 80
VT_ALL = MLA_HEADS * VT_ROWS
VT_PAD = 384
LOG2E = math.log2(math.e)
POOL_HDR = 32
CONV_HDR = 8
NEG_BIG = -1e30

TM_IN = 1024
TQ = 2048
TK = 512
TM_OUT = 1024
FF_CHUNK = 256
N_FF_CHUNKS = D_FF // FF_CHUNK


def _rms(x, g):
    return x * lax.rsqrt(jnp.mean(x * x, axis=-1, keepdims=True) + RMS_EPS) * g


def _dot(a, b):
    return lax.dot_general(a, b, (((1,), (0,)), ((), ())), preferred_element_type=jnp.float32)


def _zero_after(x):
    bits = pltpu.bitcast(x, jnp.int32)
    return lax.shift_right_logical(lax.shift_right_logical(bits, 16), 16).astype(jnp.float32)


def _dot_nt(a, b):
    return lax.dot_general(a, b, (((1,), (1,)), ((), ())), preferred_element_type=jnp.float32)


def _in_stage_kernel(x_ref, g_attn_ref, w_in_ref, g_q_ref, w_uq_ref, g_kv_ref, w_kv_ref,
                     tq_ref, tk_ref, conv_w_ref, w_pool_ref, pool_scale_ref, mixb_ref, mixc_ref,
                     q_ref, k_ref, vt_ref, yb_ref, yc_ref, qsw_ref, ksw_ref, vswt_ref,
                     conv_scr, p0, p1, p2, *, tiles_per_seq):
    tm = x_ref.shape[0]
    t = pl.program_id(0)
    tile_in_seq = t % tiles_per_seq
    pos0 = pl.multiple_of(tile_in_seq * tm, tm)

    @pl.when(tile_in_seq == 0)
    def _():
        conv_scr[0:CONV_HDR, :] = jnp.zeros((CONV_HDR, D_GROUP), jnp.float32)
        p0[0:POOL_HDR, :] = jnp.zeros((POOL_HDR, D_GROUP), jnp.float32)

    h = _rms(x_ref[...], g_attn_ref[...]).astype(jnp.bfloat16)
    proj_all = _dot(h, w_in_ref[...])
    proj = lambda off, width: proj_all[:, off:off + width]

    lane = lax.broadcasted_iota(jnp.int32, (tm, MLA_W), 1) % SLOT
    qn = _rms(proj(OFF_CQ, MLA_Q_RANK), g_q_ref[...]).astype(jnp.bfloat16)
    qa = _dot(qn, w_uq_ref[...])
    tq_tab = tq_ref[pl.ds(pos0, tm), :]
    qp = qa * jnp.concatenate([tq_tab] * MLA_HEADS, axis=1)
    q_rot = pltpu.roll(qp, MLA_W - MLA_ROPE, axis=1)
    q = jnp.where(lane < MLA_NOPE + MLA_ROPE, qp, 0.0) + jnp.where(
        (lane >= MLA_NOPE) & (lane < MLA_NOPE + MLA_ROPE), q_rot, 0.0)
    q_ref[...] = q.astype(jnp.bfloat16)

    ckn = _rms(proj(OFF_CKV, MLA_KV_RANK), g_kv_ref[...]).astype(jnp.bfloat16)
    kv = _dot(ckn, w_kv_ref[...])
    kr = proj(OFF_KR, SLOT) * tk_ref[pl.ds(pos0, tm), :]
    lane1 = lax.broadcasted_iota(jnp.int32, (tm, SLOT), 1)
    kr = jnp.where((lane1 >= MLA_NOPE) & (lane1 < MLA_NOPE + MLA_ROPE),
                   kr + pltpu.roll(kr, SLOT - MLA_ROPE, axis=1), 0.0)
    k_ref[...] = (kv[:, :MLA_W] + jnp.concatenate([kr] * MLA_HEADS, axis=1)).astype(jnp.bfloat16)

    z = proj(OFF_GC, D_GROUP) * proj(OFF_UCONV, D_GROUP)
    conv_scr[CONV_HDR:CONV_HDR + tm, :] = z
    z1 = conv_scr[CONV_HDR - 1:CONV_HDR - 1 + tm, :]
    z2 = conv_scr[CONV_HDR - 2:CONV_HDR - 2 + tm, :]
    cw = conv_w_ref[...]
    y_b = proj(OFF_GB, D_GROUP) * (cw[0:1, :] * z2 + cw[1:2, :] * z1 + cw[2:3, :] * z)
    conv_scr[0:CONV_HDR, :] = z[tm - CONV_HDR:tm, :]
    yb_ref[...] = _rms(y_b, mixb_ref[...]).astype(jnp.bfloat16)

    u = proj(OFF_UPOOL, D_GROUP)
    n = tm + POOL_HDR
    p0[POOL_HDR:n, :] = u
    p1[8:n, :] = p0[8:n, :] + p0[7:n - 1, :]
    s2 = p1[POOL_HDR:n, :]
    p2[16:n, :] = p1[16:n, :] + p1[14:n - 2, :]
    s4 = p2[POOL_HDR:n, :]
    p1[24:n, :] = p2[24:n, :] + p2[20:n - 4, :]
    s8 = p1[POOL_HDR:n, :]
    s16 = s8 + p1[24:n - 8, :]
    p0[0:POOL_HDR, :] = u[tm - POOL_HDR:tm, :]
    lane_c = lax.broadcasted_iota(jnp.int32, (tm, D_GROUP), 1)
    row_c = lax.broadcasted_iota(jnp.int32, (tm, D_GROUP), 0)
    win = jnp.where(lane_c < POOL_CH, s2, jnp.where(lane_c < 2 * POOL_CH, s4,
                    jnp.where(lane_c < 3 * POOL_CH, s8, s16)))
    width = jnp.where(lane_c < POOL_CH, POOL_WINDOWS[0], jnp.where(lane_c < 2 * POOL_CH, POOL_WINDOWS[1],
                      jnp.where(lane_c < 3 * POOL_CH, POOL_WINDOWS[2], POOL_WINDOWS[3])))
    count = jnp.minimum(pos0 + row_c + 1, width).astype(jnp.float32)
    pooled = win / count - u
    y_c = _dot(pooled.astype(jnp.bfloat16), w_pool_ref[...]) * pool_scale_ref[...]
    yc_ref[...] = _rms(y_c, mixc_ref[...]).astype(jnp.bfloat16)

    qsw_ref[...] = (proj(OFF_QSW, D_GROUP) * (LOG2E / math.sqrt(SWA_HEAD_DIM))).astype(jnp.bfloat16)
    lane_s = lax.broadcasted_iota(jnp.int32, (tm, SLOT), 1)
    a = proj(OFF_KSW, SLOT)
    r = pltpu.roll(a, SWA_HEAD_DIM, axis=1)
    ksw_ref[...] = jnp.concatenate([jnp.where(lane_s < SWA_HEAD_DIM, a, r),
                                    jnp.where(lane_s < SWA_HEAD_DIM, r, a)], axis=1).astype(jnp.bfloat16)

    lane_v = lax.broadcasted_iota(jnp.int32, (tm, VT_PAD), 1)
    ones_col = jnp.where((lane_v % VT_ROWS == MLA_V) & (lane_v < VT_ALL), 1.0, 0.0)
    v_t = jnp.concatenate([proj(OFF_VSW, SLOT), kv[:, MLA_W:] + ones_col], axis=1).T
    vswt_ref[...] = v_t[:SLOT, :].astype(jnp.bfloat16)
    vt_ref[...] = v_t[SLOT:SLOT + VT_ALL, :].astype(jnp.bfloat16)


def _in_stage(x2d, lw, tabs, seq):
    n_tok = x2d.shape[0]
    tm = TM_IN
    tiles_per_seq = seq // tm
    const = lambda shape: pl.BlockSpec(shape, lambda t: (0, 0), pipeline_mode=pl.Buffered(1))
    tile = lambda w: pl.BlockSpec((tm, w), lambda t: (t, 0))
    vt_spec = lambda rows: pl.BlockSpec((None, rows, tm), lambda t: (t // tiles_per_seq, 0, t % tiles_per_seq))
    bf = jnp.bfloat16
    tok = lambda w: jax.ShapeDtypeStruct((n_tok, w), bf)
    out_shape = [tok(MLA_W), tok(MLA_W), jax.ShapeDtypeStruct((n_tok // seq, VT_ALL, seq), bf),
                 tok(D_GROUP), tok(D_GROUP), tok(D_GROUP), tok(D_GROUP),
                 jax.ShapeDtypeStruct((n_tok // seq, SLOT, seq), bf)]
    return pl.pallas_call(
        functools.partial(_in_stage_kernel, tiles_per_seq=tiles_per_seq),
        grid=(n_tok // tm,),
        in_specs=[tile(D_MODEL), const((1, D_MODEL)), const((D_MODEL, D_IN_PAD)),
                  const((1, MLA_Q_RANK)), const((MLA_Q_RANK, MLA_W)),
                  const((1, MLA_KV_RANK)), const((MLA_KV_RANK, MLA_W + VT_PAD)),
                  const((seq, SLOT)), const((seq, SLOT)),
                  const((CONV_WIDTH, D_GROUP)), const((D_GROUP, D_GROUP)), const((1, D_GROUP)),
                  const((1, D_GROUP)), const((1, D_GROUP))],
        out_specs=[tile(MLA_W), tile(MLA_W), vt_spec(VT_ALL), tile(D_GROUP), tile(D_GROUP),
                   tile(D_GROUP), tile(D_GROUP), vt_spec(SLOT)],
        out_shape=out_shape,
        scratch_shapes=[pltpu.VMEM((tm + CONV_HDR, D_GROUP), jnp.float32),
                        pltpu.VMEM((tm + POOL_HDR, D_GROUP), jnp.float32),
                        pltpu.VMEM((tm + POOL_HDR, D_GROUP), jnp.float32),
                        pltpu.VMEM((tm + POOL_HDR, D_GROUP), jnp.float32)],
        compiler_params=pltpu.CompilerParams(dimension_semantics=("arbitrary",),
                                             vmem_limit_bytes=VMEM_LIMIT_IN_STAGE),
        name="in_stage",
    )(x2d, lw["g_attn"], lw["w_in"], lw["g_q"], lw["w_uq"], lw["g_kv"], lw["w_kv"],
      tabs["tq"], tabs["tk"], lw["conv_w"], lw["w_pool"], lw["pool_scale"], lw["mix_b"], lw["mix_c"])


def _mla_kernel(q_ref, k_ref, vt_ref, mix_ref, o_ref, acc_scr, s_scr):
    tq = q_ref.shape[0]
    i = pl.program_id(1)
    acc_scr[...] = jnp.zeros(acc_scr.shape, jnp.float32)

    def scores(h, cols, key_start):
        k_h = k_ref[pl.ds(key_start, TK), h * SLOT:(h + 1) * SLOT]
        return _dot_nt(k_h, q_ref[cols, h * SLOT:(h + 1) * SLOT])

    def accumulate(h, s, m_old, cols, key_start, masked):
        if masked:
            head = jnp.where(causal, s[:, :TK], NEG_BIG)
            s = jnp.concatenate([head, s[:, TK:]], axis=1) if s.shape[1] > TK else head
        m_new = jnp.maximum(m_old, jnp.max(s, axis=0, keepdims=True))
        p = jnp.exp2(s - m_new).astype(jnp.bfloat16)
        vt_h = vt_ref[h * VT_ROWS:(h + 1) * VT_ROWS, pl.ds(key_start, TK)]
        acc_scr[h, :, cols] = jnp.exp2(m_old - m_new) * acc_scr[h, :, cols] + _dot(vt_h, p)
        return m_new

    def run_units(units, m, following):
        m = list(m)
        s_next = s_scr[...]
        for u, (h, cols, key_start, masked) in enumerate(units):
            s_cur = s_next
            nxt = units[u + 1] if u + 1 < len(units) else following
            if nxt is not None:
                s_next = scores(*nxt[:3])
            ncol = cols.stop - cols.start
            m_new = accumulate(h, s_cur, m[h][:, m[h].shape[1] - ncol:], cols, key_start, masked)
            m[h] = m_new
        if following is not None:
            s_scr[...] = s_next
        return tuple(m)

    all_cols = slice(0, tq)
    tiles_per_step = tq // TK
    tile_start = lambda t: pl.multiple_of(t * TK, TK)

    def full_tiles(j, m):
        units = []
        for t in range(tiles_per_step):
            units += [(h, all_cols, tile_start(j * tiles_per_step + t), False) for h in range(MLA_HEADS)]
        return run_units(units, m, following=(0, all_cols, tile_start((j + 1) * tiles_per_step)))

    s_scr[...] = scores(0, all_cols, 0)
    m = tuple(jnp.full((1, tq), NEG_BIG, jnp.float32) for _ in range(MLA_HEADS))
    m = lax.fori_loop(0, i, full_tiles, m)

    causal = lax.broadcasted_iota(jnp.int32, (TK, TK), 0) <= lax.broadcasted_iota(jnp.int32, (TK, TK), 1)
    units = []
    for d in range(tiles_per_step):
        key_start = pl.multiple_of(i * tq + d * TK, TK)
        units += [(h, slice(d * TK, tq), key_start, True) for h in range(MLA_HEADS)]
    run_units(units, m, following=None)

    y_t = jnp.concatenate([acc_scr[h, 0:MLA_V, :] / acc_scr[h, MLA_V:MLA_V + 1, :] for h in range(MLA_HEADS)], axis=0)
    o_ref[...] = _rms(y_t.T, mix_ref[...]).astype(jnp.bfloat16)


def _mla_attention(q, k, vt, mix_a, batch, seq):
    nq = seq // TQ
    return pl.pallas_call(
        _mla_kernel,
        grid=(batch, nq),
        in_specs=[pl.BlockSpec((TQ, MLA_W), lambda b, i: (b * nq + i, 0)),
                  pl.BlockSpec((seq, MLA_W), lambda b, i: (b, 0)),
                  pl.BlockSpec((None, VT_ALL, seq), lambda b, i: (b, 0, 0)),
                  pl.BlockSpec((1, D_GROUP), lambda b, i: (0, 0))],
        out_specs=pl.BlockSpec((TQ, D_GROUP), lambda b, i: (b * nq + i, 0)),
        out_shape=jax.ShapeDtypeStruct((batch * seq, D_GROUP), jnp.bfloat16),
        scratch_shapes=[pltpu.VMEM((MLA_HEADS, VT_ROWS, TQ), jnp.float32),
                        pltpu.VMEM((TK, TQ), jnp.float32)],
        compiler_params=pltpu.CompilerParams(dimension_semantics=("arbitrary", "arbitrary"),
                                             vmem_limit_bytes=VMEM_LIMIT_ATTENTION),
        name="mla_attention",
    )(q, k, vt, mix_a)


def _swa_kernel(sinks_ref, q_ref, k_ref, vt_ref, bias_ref, mix_ref, o_ref, yt_scr):
    tq = q_ref.shape[0]
    blk = SWA_WINDOW
    i = pl.program_id(1)
    lane_q = lax.broadcasted_iota(jnp.int32, (blk, D_GROUP), 1)
    head_of_col = lax.broadcasted_iota(jnp.int32, (1, SWA_HEADS * blk), 1) // blk
    sink_row = jnp.zeros((1, SWA_HEADS * blk), jnp.float32)
    for h in range(SWA_HEADS):
        sink_row = jnp.where(head_of_col == h, sinks_ref[h] * LOG2E, sink_row)

    def key_start(jb):
        return pl.multiple_of(jnp.maximum(i * tq + (jb - 1) * blk, 0), blk)

    def scores(jb):
        k_t = k_ref[pl.ds(key_start(jb), 2 * blk), :]
        q_b = q_ref[jb * blk:(jb + 1) * blk, :]
        q_stack = jnp.concatenate(
            [jnp.where((lane_q >= h * SWA_HEAD_DIM) & (lane_q < (h + 1) * SWA_HEAD_DIM), q_b, 0)
             for h in range(SWA_HEADS)], axis=0)
        return _dot_nt(k_t, q_stack)

    n_blk = tq // blk
    s_next = scores(0)
    for jb in range(n_blk):
        s_cur = s_next
        if jb + 1 < n_blk:
            s_next = scores(jb + 1)
        vt_t = vt_ref[:, pl.ds(key_start(jb), 2 * blk)]
        s = s_cur + bias_ref[jnp.minimum(i * tq + jb * blk, 1)]
        m = jnp.maximum(jnp.max(s, axis=0, keepdims=True), sink_row)
        if jb + 1 < n_blk:
            m = m + _zero_after(s_next[0:1, :])
        p = jnp.exp2(s - m)
        inv = 1.0 / (jnp.sum(p, axis=0, keepdims=True) + jnp.exp2(sink_row - m))
        o = _dot(vt_t, p.astype(jnp.bfloat16)) * inv
        for h in range(SWA_HEADS):
            yt_scr[h * SWA_HEAD_DIM:(h + 1) * SWA_HEAD_DIM, jb * blk:(jb + 1) * blk] = (
                o[(h // 2) * SWA_HEAD_DIM:(h // 2 + 1) * SWA_HEAD_DIM, h * blk:(h + 1) * blk])
    o_ref[...] = _rms(yt_scr[...].T, mix_ref[...]).astype(jnp.bfloat16)


def _swa_bias_tables(slopes):
    blk = SWA_WINDOW
    key = np.arange(2 * blk)[:, None]
    qry = np.arange(blk)[None, :]
    tabs = []
    for off in (0, blk):
        dist = off + qry - key
        valid = (dist >= 0) & (dist < SWA_WINDOW)
        tabs.append(np.concatenate([np.where(valid, -s * LOG2E * dist, NEG_BIG) for s in slopes], axis=1))
    return jnp.asarray(np.stack(tabs), jnp.float32)


def _swa_attention(sinks, q, k, vt, bias, mix_d, batch, seq):
    tq = 4096
    nq = seq // tq
    blk = SWA_WINDOW
    return pl.pallas_call(
        _swa_kernel,
        grid=(batch, nq),
        in_specs=[pl.BlockSpec(memory_space=pltpu.SMEM),
                  pl.BlockSpec((tq, D_GROUP), lambda b, i: (b * nq + i, 0)),
                  pl.BlockSpec((seq, D_GROUP), lambda b, i: (b, 0)),
                  pl.BlockSpec((None, SLOT, seq), lambda b, i: (b, 0, 0)),
                  pl.BlockSpec((2, 2 * blk, SWA_HEADS * blk), lambda b, i: (0, 0, 0)),
                  pl.BlockSpec((1, D_GROUP), lambda b, i: (0, 0))],
        out_specs=pl.BlockSpec((tq, D_GROUP), lambda b, i: (b * nq + i, 0)),
        out_shape=jax.ShapeDtypeStruct((batch * seq, D_GROUP), jnp.bfloat16),
        scratch_shapes=[pltpu.VMEM((D_GROUP, tq), jnp.float32)],
        compiler_params=pltpu.CompilerParams(dimension_semantics=("arbitrary", "arbitrary"),
                                             vmem_limit_bytes=VMEM_LIMIT_ATTENTION),
        name="swa_attention",
    )(sinks, q, k, vt, bias, mix_d)


def _out_stage_kernel(x_ref, ya_ref, yb_ref, yc_ref, yd_ref, w_o_ref, g_ffn_ref, w_gu_ref,
                      w_down_ref, g_final_ref, o_ref, act_scr, *, final):
    x = x_ref[...]
    for g, y_ref in enumerate((ya_ref, yb_ref, yc_ref, yd_ref)):
        x = x + _dot(y_ref[...], w_o_ref[g * D_GROUP:(g + 1) * D_GROUP, :])
    h2 = _rms(x, g_ffn_ref[...]).astype(jnp.bfloat16)

    for c in range(N_FF_CHUNKS):
        cols = slice(c * FF_CHUNK, (c + 1) * FF_CHUNK)
        gate = _dot(h2, w_gu_ref[:, cols])
        up = _dot(h2, w_gu_ref[:, D_FF + c * FF_CHUNK:D_FF + (c + 1) * FF_CHUNK])
        act_scr[:, cols] = (gate * jax.nn.sigmoid(gate) * up).astype(jnp.bfloat16)
    x = x + _dot(act_scr[...], w_down_ref[...])
    if final:
        x = _rms(x, g_final_ref[...])
    o_ref[...] = x


def _out_stage(x2d, ya, yb, yc, yd, lw, stacked, l, g_final, final):
    n_tok = x2d.shape[0]
    tm = TM_OUT
    const = lambda shape: pl.BlockSpec(shape, lambda t: (0,) * len(shape), pipeline_mode=pl.Buffered(1))
    of_layer = lambda shape: pl.BlockSpec((None,) + shape, lambda t: (l,) + (0,) * len(shape),
                                          pipeline_mode=pl.Buffered(1))
    tile = lambda w: pl.BlockSpec((tm, w), lambda t: (t, 0))
    return pl.pallas_call(
        functools.partial(_out_stage_kernel, final=final),
        grid=(n_tok // tm,),
        in_specs=[tile(D_MODEL), tile(D_GROUP), tile(D_GROUP), tile(D_GROUP), tile(D_GROUP),
                  of_layer((D_MODEL, D_MODEL)), const((1, D_MODEL)), of_layer((D_MODEL, 2 * D_FF)),
                  of_layer((D_FF, D_MODEL)), const((1, D_MODEL))],
        out_specs=tile(D_MODEL),
        out_shape=jax.ShapeDtypeStruct((n_tok, D_MODEL), jnp.float32),
        scratch_shapes=[pltpu.VMEM((tm, D_FF), jnp.bfloat16)],
        compiler_params=pltpu.CompilerParams(dimension_semantics=("arbitrary",),
                                             vmem_limit_bytes=VMEM_LIMIT_OUT_STAGE),
        name="out_stage",
    )(x2d, ya, yb, yc, yd, stacked["w_o"], lw["g_ffn"], stacked["w_gate_up"], stacked["w_down"], g_final)


def _rope_tables(seq):
    inv = 1.0 / (ROPE_THETA ** (jnp.arange(0, MLA_ROPE, 2, dtype=jnp.float32) / MLA_ROPE))
    ang = jnp.arange(seq, dtype=jnp.float32)[:, None] * inv[None, :]
    cos, sin = jnp.cos(ang), jnp.sin(ang)
    cos2 = jnp.concatenate([cos, cos], axis=1)
    sin2 = jnp.concatenate([sin, sin], axis=1)
    scale = LOG2E / math.sqrt(MLA_NOPE + MLA_ROPE)
    tq = jnp.concatenate([jnp.full((seq, MLA_NOPE), scale, jnp.float32), cos2 * scale, sin2 * scale], axis=1)
    tk = jnp.concatenate([jnp.zeros((seq, MLA_NOPE), jnp.float32), cos2, sin2], axis=1)
    return {"tq": tq, "tk": tk}


def _swap_halves(w):
    half = w.shape[-1] // 2
    return jnp.concatenate([-w[..., half:], w[..., :half]], axis=-1)


def _layer_weights(l, attn_norm, w_in, mla_q_norm, w_uq, mla_kv_norm, w_ukv, conv_w, pool_w, pool_scale,
                   mix_norm, ffn_norm):
    bf = jnp.bfloat16
    f32 = jnp.float32
    wi = w_in[l]
    pts = np.cumsum((0, 256, 128, 32, 256, 256, 256, 256, 256, 128, 128))
    c_q, c_kv, k_r, g_b, g_c, u_conv, u_pool, q_sw, k_sw, v_sw = [wi[:, pts[j]:pts[j + 1]] for j in range(10)]
    zeros = lambda w: jnp.zeros((D_MODEL, w), f32)
    w_in_r = jnp.concatenate([c_q, c_kv, zeros(MLA_NOPE), k_r, _swap_halves(k_r),
                              g_b, g_c, u_conv, u_pool, q_sw, k_sw, v_sw], axis=1)
    wq = w_uq[l].reshape(MLA_Q_RANK, MLA_HEADS, MLA_NOPE + MLA_ROPE)
    wq_rot = wq[..., MLA_NOPE:]
    w_uq_p = jnp.concatenate([wq, _swap_halves(wq_rot)], axis=-1).reshape(MLA_Q_RANK, MLA_W)
    wkv = w_ukv[l].reshape(MLA_KV_RANK, MLA_HEADS, MLA_NOPE + MLA_V)
    zk = jnp.zeros((MLA_KV_RANK, MLA_HEADS, SLOT - MLA_NOPE), f32)
    w_k = jnp.concatenate([wkv[..., :MLA_NOPE], zk], axis=-1).reshape(MLA_KV_RANK, MLA_W)
    zv = jnp.zeros((MLA_KV_RANK, MLA_HEADS, VT_ROWS - MLA_V), f32)
    w_v = jnp.concatenate([wkv[..., MLA_NOPE:], zv], axis=-1).reshape(MLA_KV_RANK, VT_ALL)
    w_v = jnp.concatenate([w_v, jnp.zeros((MLA_KV_RANK, VT_PAD - VT_ALL), f32)], axis=1)
    w_pool = jax.scipy.linalg.block_diag(*[pool_w[l, g] for g in range(len(POOL_WINDOWS))])
    mix = mix_norm[l].reshape(4, 1, D_GROUP)
    return {
        "g_attn": attn_norm[l][None, :], "w_in": w_in_r.astype(bf),
        "g_q": mla_q_norm[l][None, :], "w_uq": w_uq_p.astype(bf),
        "g_kv": mla_kv_norm[l][None, :], "w_kv": jnp.concatenate([w_k, w_v], axis=1).astype(bf),
        "conv_w": conv_w[l], "w_pool": w_pool.astype(bf), "pool_scale": pool_scale[l][None, :],
        "mix_a": mix[0], "mix_b": mix[1], "mix_c": mix[2], "mix_d": mix[3],
        "g_ffn": ffn_norm[l][None, :],
    }


def kernel(x, attn_norm, w_in, mla_q_norm, w_uq, mla_kv_norm, w_ukv, conv_w, pool_w, pool_scale, swa_sinks,
           mix_norm, w_o, ffn_norm, w_gate_up, w_down, final_norm):
    batch, seq, d_model = x.shape
    depth = w_in.shape[0]
    assert d_model == D_MODEL and w_in.shape[2] == D_IN
    assert seq % TM_IN == 0 and seq % TQ == 0 and TQ % TK == 0 and (batch * seq) % TM_OUT == 0
    slopes = tuple(float(2.0 ** (-8.0 * (h + 1) / SWA_HEADS)) for h in range(SWA_HEADS))
    tabs = _rope_tables(seq)
    swa_bias = _swa_bias_tables(slopes)
    x2d = x.reshape(batch * seq, D_MODEL)
    g_final = final_norm[None, :]
    stacked = {"w_o": w_o.astype(jnp.bfloat16), "w_gate_up": w_gate_up.astype(jnp.bfloat16),
               "w_down": w_down.astype(jnp.bfloat16)}
    for l in range(depth):
        lw = _layer_weights(l, attn_norm, w_in, mla_q_norm, w_uq, mla_kv_norm, w_ukv, conv_w, pool_w,
                            pool_scale, mix_norm, ffn_norm)
        q, k, vt, yb, yc, qsw, ksw, vswt = _in_stage(x2d, lw, tabs, seq)
        ya = _mla_attention(q, k, vt, lw["mix_a"], batch, seq)
        yd = _swa_attention(swa_sinks[l], qsw, ksw, vswt, swa_bias, lw["mix_d"], batch, seq)
        x2d = _out_stage(x2d, ya, yb, yc, yd, lw, stacked, l, g_final, final=(l == depth - 1))
    return x2d.reshape(batch, seq, D_MODEL)
```

```python
import functools
import math

import jax
import jax.numpy as jnp
import numpy as np
from jax import lax
from jax.experimental import pallas as pl
from jax.experimental.pallas import tpu as pltpu

D_MODEL = 1024
D_GROUP = 256
MLA_HEADS = 4
MLA_Q_RANK = 256
MLA_KV_RANK = 128
MLA_NOPE = 64
MLA_ROPE = 32
MLA_V = 64
ROPE_THETA = 10000.0
CONV_WIDTH = 3
POOL_WINDOWS = (2, 4, 8, 16)
POOL_CH = 64
SWA_HEADS = 4
SWA_KV_HEADS = 2
SWA_HEAD_DIM = 64
SWA_WINDOW = 128
D_FF = 2816
RMS_EPS = 1e-6
D_IN = 1952

LANES = 128
MIB = 1024 * 1024
VMEM_LIMIT_IN_STAGE = 40 * MIB
VMEM_LIMIT_ATTENTION = 40 * MIB
VMEM_LIMIT_OUT_STAGE = 52 * MIB

D_IN_PAD = 2048
OFF_CQ, OFF_CKV, OFF_KR, OFF_GB, OFF_GC, OFF_UCONV, OFF_UPOOL = 0, 256, 384, 512, 768, 1024, 1280
OFF_QSW, OFF_KSW, OFF_VSW = 1536, 1792, 1920
SLOT = LANES
MLA_W = MLA_HEADS * SLOT
VT_ROWS = 80
VT_ALL = MLA_HEADS * VT_ROWS
VT_PAD = 384
LOG2E = math.log2(math.e)
POOL_HDR = 32
CONV_HDR = 8
NEG_BIG = -1e30

TM_IN = 1024
TQ = 2048
TK = 512
TM_OUT = 1024
FF_CHUNK = 256
N_FF_CHUNKS = D_FF // FF_CHUNK


def _rms(x, g):
    return x * lax.rsqrt(jnp.mean(x * x, axis=-1, keepdims=True) + RMS_EPS) * g


def _dot(a, b):
    return lax.dot_general(a, b, (((1,), (0,)), ((), ())), preferred_element_type=jnp.float32)


def _zero_after(x):
    bits = pltpu.bitcast(x, jnp.int32)
    return lax.shift_right_logical(lax.shift_right_logical(bits, 16), 16).astype(jnp.float32)


def _dot_nt(a, b):
    return lax.dot_general(a, b, (((1,), (1,)), ((), ())), preferred_element_type=jnp.float32)


def _in_stage_kernel(x_ref, g_attn_ref, w_in_ref, g_q_ref, w_uq_ref, g_kv_ref, w_kv_ref,
                     tq_ref, tk_ref, conv_w_ref, w_pool_ref, pool_scale_ref, mixb_ref, mixc_ref,
                     q_ref, k_ref, vt_ref, yb_ref, yc_ref, qsw_ref, ksw_ref, vswt_ref,
                     conv_scr, p0, p1, p2, *, tiles_per_seq):
    tm = x_ref.shape[0]
    t = pl.program_id(0)
    tile_in_seq = t % tiles_per_seq
    pos0 = pl.multiple_of(tile_in_seq * tm, tm)

    @pl.when(tile_in_seq == 0)
    def _():
        conv_scr[0:CONV_HDR, :] = jnp.zeros((CONV_HDR, D_GROUP), jnp.float32)
        p0[0:POOL_HDR, :] = jnp.zeros((POOL_HDR, D_GROUP), jnp.float32)

    h = _rms(x_ref[...], g_attn_ref[...]).astype(jnp.bfloat16)
    proj_all = _dot(h, w_in_ref[...])
    proj = lambda off, width: proj_all[:, off:off + width]

    lane = lax.broadcasted_iota(jnp.int32, (tm, MLA_W), 1) % SLOT
    qn = _rms(proj(OFF_CQ, MLA_Q_RANK), g_q_ref[...]).astype(jnp.bfloat16)
    qa = _dot(qn, w_uq_ref[...])
    tq_tab = tq_ref[pl.ds(pos0, tm), :]
    qp = qa * jnp.concatenate([tq_tab] * MLA_HEADS, axis=1)
    q_rot = pltpu.roll(qp, MLA_W - MLA_ROPE, axis=1)
    q = jnp.where(lane < MLA_NOPE + MLA_ROPE, qp, 0.0) + jnp.where(
        (lane >= MLA_NOPE) & (lane < MLA_NOPE + MLA_ROPE), q_rot, 0.0)
    q_ref[...] = q.astype(jnp.bfloat16)

    ckn = _rms(proj(OFF_CKV, MLA_KV_RANK), g_kv_ref[...]).astype(jnp.bfloat16)
    kv = _dot(ckn, w_kv_ref[...])
    kr = proj(OFF_KR, SLOT) * tk_ref[pl.ds(pos0, tm), :]
    lane1 = lax.broadcasted_iota(jnp.int32, (tm, SLOT), 1)
    kr = jnp.where((lane1 >= MLA_NOPE) & (lane1 < MLA_NOPE + MLA_ROPE),
                   kr + pltpu.roll(kr, SLOT - MLA_ROPE, axis=1), 0.0)
    k_ref[...] = (kv[:, :MLA_W] + jnp.concatenate([kr] * MLA_HEADS, axis=1)).astype(jnp.bfloat16)

    z = proj(OFF_GC, D_GROUP) * proj(OFF_UCONV, D_GROUP)
    conv_scr[CONV_HDR:CONV_HDR + tm, :] = z
    z1 = conv_scr[CONV_HDR - 1:CONV_HDR - 1 + tm, :]
    z2 = conv_scr[CONV_HDR - 2:CONV_HDR - 2 + tm, :]
    cw = conv_w_ref[...]
    y_b = proj(OFF_GB, D_GROUP) * (cw[0:1, :] * z2 + cw[1:2, :] * z1 + cw[2:3, :] * z)
    conv_scr[0:CONV_HDR, :] = z[tm - CONV_HDR:tm, :]
    yb_ref[...] = _rms(y_b, mixb_ref[...]).astype(jnp.bfloat16)

    u = proj(OFF_UPOOL, D_GROUP)
    n = tm + POOL_HDR
    p0[POOL_HDR:n, :] = u
    p1[8:n, :] = p0[8:n, :] + p0[7:n - 1, :]
    s2 = p1[POOL_HDR:n, :]
    p2[16:n, :] = p1[16:n, :] + p1[14:n - 2, :]
    s4 = p2[POOL_HDR:n, :]
    p1[24:n, :] = p2[24:n, :] + p2[20:n - 4, :]
    s8 = p1[POOL_HDR:n, :]
    s16 = s8 + p1[24:n - 8, :]
    p0[0:POOL_HDR, :] = u[tm - POOL_HDR:tm, :]
    lane_c = lax.broadcasted_iota(jnp.int32, (tm, D_GROUP), 1)
    row_c = lax.broadcasted_iota(jnp.int32, (tm, D_GROUP), 0)
    win = jnp.where(lane_c < POOL_CH, s2, jnp.where(lane_c < 2 * POOL_CH, s4,
                    jnp.where(lane_c < 3 * POOL_CH, s8, s16)))
    width = jnp.where(lane_c < POOL_CH, POOL_WINDOWS[0], jnp.where(lane_c < 2 * POOL_CH, POOL_WINDOWS[1],
                      jnp.where(lane_c < 3 * POOL_CH, POOL_WINDOWS[2], POOL_WINDOWS[3])))
    count = jnp.minimum(pos0 + row_c + 1, width).astype(jnp.float32)
    pooled = win / count - u
    y_c = _dot(pooled.astype(jnp.bfloat16), w_pool_ref[...]) * pool_scale_ref[...]
    yc_ref[...] = _rms(y_c, mixc_ref[...]).astype(jnp.bfloat16)

    qsw_ref[...] = (proj(OFF_QSW, D_GROUP) * (LOG2E / math.sqrt(SWA_HEAD_DIM))).astype(jnp.bfloat16)
    lane_s = lax.broadcasted_iota(jnp.int32, (tm, SLOT), 1)
    a = proj(OFF_KSW, SLOT)
    r = pltpu.roll(a, SWA_HEAD_DIM, axis=1)
    ksw_ref[...] = jnp.concatenate([jnp.where(lane_s < SWA_HEAD_DIM, a, r),
                                    jnp.where(lane_s < SWA_HEAD_DIM, r, a)], axis=1).astype(jnp.bfloat16)

    lane_v = lax.broadcasted_iota(jnp.int32, (tm, VT_PAD), 1)
    ones_col = jnp.where((lane_v % VT_ROWS == MLA_V) & (lane_v < VT_ALL), 1.0, 0.0)
    v_t = jnp.concatenate([proj(OFF_VSW, SLOT), kv[:, MLA_W:] + ones_col], axis=1).T
    vswt_ref[...] = v_t[:SLOT, :].astype(jnp.bfloat16)
    vt_ref[...] = v_t[SLOT:SLOT + VT_ALL, :].astype(jnp.bfloat16)


def _in_stage(x2d, lw, tabs, seq):
    n_tok = x2d.shape[0]
    tm = TM_IN
    tiles_per_seq = seq // tm
    const = lambda shape: pl.BlockSpec(shape, lambda t: (0, 0), pipeline_mode=pl.Buffered(1))
    tile = lambda w: pl.BlockSpec((tm, w), lambda t: (t, 0))
    vt_spec = lambda rows: pl.BlockSpec((None, rows, tm), lambda t: (t // tiles_per_seq, 0, t % tiles_per_seq))
    bf = jnp.bfloat16
    tok = lambda w: jax.ShapeDtypeStruct((n_tok, w), bf)
    out_shape = [tok(MLA_W), tok(MLA_W), jax.ShapeDtypeStruct((n_tok // seq, VT_ALL, seq), bf),
                 tok(D_GROUP), tok(D_GROUP), tok(D_GROUP), tok(D_GROUP),
                 jax.ShapeDtypeStruct((n_tok // seq, SLOT, seq), bf)]
    return pl.pallas_call(
        functools.partial(_in_stage_kernel, tiles_per_seq=tiles_per_seq),
        grid=(n_tok // tm,),
        in_specs=[tile(D_MODEL), const((1, D_MODEL)), const((D_MODEL, D_IN_PAD)),
                  const((1, MLA_Q_RANK)), const((MLA_Q_RANK, MLA_W)),
                  const((1, MLA_KV_RANK)), const((MLA_KV_RANK, MLA_W + VT_PAD)),
                  const((seq, SLOT)), const((seq, SLOT)),
                  const((CONV_WIDTH, D_GROUP)), const((D_GROUP, D_GROUP)), const((1, D_GROUP)),
                  const((1, D_GROUP)), const((1, D_GROUP))],
        out_specs=[tile(MLA_W), tile(MLA_W), vt_spec(VT_ALL), tile(D_GROUP), tile(D_GROUP),
                   tile(D_GROUP), tile(D_GROUP), vt_spec(SLOT)],
        out_shape=out_shape,
        scratch_shapes=[pltpu.VMEM((tm + CONV_HDR, D_GROUP), jnp.float32),
                        pltpu.VMEM((tm + POOL_HDR, D_GROUP), jnp.float32),
                        pltpu.VMEM((tm + POOL_HDR, D_GROUP), jnp.float32),
                        pltpu.VMEM((tm + POOL_HDR, D_GROUP), jnp.float32)],
        compiler_params=pltpu.CompilerParams(dimension_semantics=("arbitrary",),
                                             vmem_limit_bytes=VMEM_LIMIT_IN_STAGE),
        name="in_stage",
    )(x2d, lw["g_attn"], lw["w_in"], lw["g_q"], lw["w_uq"], lw["g_kv"], lw["w_kv"],
      tabs["tq"], tabs["tk"], lw["conv_w"], lw["w_pool"], lw["pool_scale"], lw["mix_b"], lw["mix_c"])


def _mla_kernel(q_ref, k_ref, vt_ref, mix_ref, o_ref, acc_scr, s_scr):
    tq = q_ref.shape[0]
    i = pl.program_id(1)
    acc_scr[...] = jnp.zeros(acc_scr.shape, jnp.float32)

    def scores(h, cols, key_start):
        k_h = k_ref[pl.ds(key_start, TK), h * SLOT:(h + 1) * SLOT]
        return _dot_nt(k_h, q_ref[cols, h * SLOT:(h + 1) * SLOT])

    def accumulate(h, s, m_old, cols, key_start, masked):
        if masked:
            head = jnp.where(causal, s[:, :TK], NEG_BIG)
            s = jnp.concatenate([head, s[:, TK:]], axis=1) if s.shape[1] > TK else head
        m_new = jnp.maximum(m_old, jnp.max(s, axis=0, keepdims=True))
        p = jnp.exp2(s - m_new).astype(jnp.bfloat16)
        vt_h = vt_ref[h * VT_ROWS:(h + 1) * VT_ROWS, pl.ds(key_start, TK)]
        acc_scr[h, :, cols] = jnp.exp2(m_old - m_new) * acc_scr[h, :, cols] + _dot(vt_h, p)
        return m_new

    def run_units(units, m, following):
        m = list(m)
        s_next = s_scr[...]
        for u, (h, cols, key_start, masked) in enumerate(units):
            s_cur = s_next
            nxt = units[u + 1] if u + 1 < len(units) else following
            if nxt is not None:
                s_next = scores(*nxt[:3])
            ncol = cols.stop - cols.start
            m_new = accumulate(h, s_cur, m[h][:, m[h].shape[1] - ncol:], cols, key_start, masked)
            m[h] = m_new
        if following is not None:
            s_scr[...] = s_next
        return tuple(m)

    all_cols = slice(0, tq)
    tiles_per_step = tq // TK
    tile_start = lambda t: pl.multiple_of(t * TK, TK)

    def full_tiles(j, m):
        units = []
        for t in range(tiles_per_step):
            units += [(h, all_cols, tile_start(j * tiles_per_step + t), False) for h in range(MLA_HEADS)]
        return run_units(units, m, following=(0, all_cols, tile_start((j + 1) * tiles_per_step)))

    s_scr[...] = scores(0, all_cols, 0)
    m = tuple(jnp.full((1, tq), NEG_BIG, jnp.float32) for _ in range(MLA_HEADS))
    m = lax.fori_loop(0, i, full_tiles, m)

    causal = lax.broadcasted_iota(jnp.int32, (TK, TK), 0) <= lax.broadcasted_iota(jnp.int32, (TK, TK), 1)
    units = []
    for d in range(tiles_per_step):
        key_start = pl.multiple_of(i * tq + d * TK, TK)
        units += [(h, slice(d * TK, tq), key_start, True) for h in range(MLA_HEADS)]
    run_units(units, m, following=None)

    y_t = jnp.concatenate([acc_scr[h, 0:MLA_V, :] / acc_scr[h, MLA_V:MLA_V + 1, :] for h in range(MLA_HEADS)], axis=0)
    o_ref[...] = _rms(y_t.T, mix_ref[...]).astype(jnp.bfloat16)


def _mla_attention(q, k, vt, mix_a, batch, seq):
    nq = seq // TQ
    return pl.pallas_call(
        _mla_kernel,
        grid=(batch, nq),
        in_specs=[pl.BlockSpec((TQ, MLA_W), lambda b, i: (b * nq + i, 0)),
                  pl.BlockSpec((seq, MLA_W), lambda b, i: (b, 0)),
                  pl.BlockSpec((None, VT_ALL, seq), lambda b, i: (b, 0, 0)),
                  pl.BlockSpec((1, D_GROUP), lambda b, i: (0, 0))],
        out_specs=pl.BlockSpec((TQ, D_GROUP), lambda b, i: (b * nq + i, 0)),
        out_shape=jax.ShapeDtypeStruct((batch * seq, D_GROUP), jnp.bfloat16),
        scratch_shapes=[pltpu.VMEM((MLA_HEADS, VT_ROWS, TQ), jnp.float32),
                        pltpu.VMEM((TK, TQ), jnp.float32)],
        compiler_params=pltpu.CompilerParams(dimension_semantics=("arbitrary", "arbitrary"),
                                             vmem_limit_bytes=VMEM_LIMIT_ATTENTION),
        name="mla_attention",
    )(q, k, vt, mix_a)


def _swa_kernel(sinks_ref, q_ref, k_ref, vt_ref, bias_ref, mix_ref, o_ref, yt_scr):
    tq = q_ref.shape[0]
    blk = SWA_WINDOW
    i = pl.program_id(1)
    lane_q = lax.broadcasted_iota(jnp.int32, (blk, D_GROUP), 1)
    head_of_col = lax.broadcasted_iota(jnp.int32, (1, SWA_HEADS * blk), 1) // blk
    sink_row = jnp.zeros((1, SWA_HEADS * blk), jnp.float32)
    for h in range(SWA_HEADS):
        sink_row = jnp.where(head_of_col == h, sinks_ref[h] * LOG2E, sink_row)

    def key_start(jb):
        return pl.multiple_of(jnp.maximum(i * tq + (jb - 1) * blk, 0), blk)

    def scores(jb):
        k_t = k_ref[pl.ds(key_start(jb), 2 * blk), :]
        q_b = q_ref[jb * blk:(jb + 1) * blk, :]
        q_stack = jnp.concatenate(
            [jnp.where((lane_q >= h * SWA_HEAD_DIM) & (lane_q < (h + 1) * SWA_HEAD_DIM), q_b, 0)
             for h in range(SWA_HEADS)], axis=0)
        return _dot_nt(k_t, q_stack)

    n_blk = tq // blk
    s_next = scores(0)
    for jb in range(n_blk):
        s_cur = s_next
        if jb + 1 < n_blk:
            s_next = scores(jb + 1)
        vt_t = vt_ref[:, pl.ds(key_start(jb), 2 * blk)]
        s = s_cur + bias_ref[jnp.minimum(i * tq + jb * blk, 1)]
        m = jnp.maximum(jnp.max(s, axis=0, keepdims=True), sink_row)
        if jb + 1 < n_blk:
            m = m + _zero_after(s_next[0:1, :])
        p = jnp.exp2(s - m)
        inv = 1.0 / (jnp.sum(p, axis=0, keepdims=True) + jnp.exp2(sink_row - m))
        o = _dot(vt_t, p.astype(jnp.bfloat16)) * inv
        for h in range(SWA_HEADS):
            yt_scr[h * SWA_HEAD_DIM:(h + 1) * SWA_HEAD_DIM, jb * blk:(jb + 1) * blk] = (
                o[(h // 2) * SWA_HEAD_DIM:(h // 2 + 1) * SWA_HEAD_DIM, h * blk:(h + 1) * blk])
    o_ref[...] = _rms(yt_scr[...].T, mix_ref[...]).astype(jnp.bfloat16)


def _swa_bias_tables(slopes):
    blk = SWA_WINDOW
    key = np.arange(2 * blk)[:, None]
    qry = np.arange(blk)[None, :]
    tabs = []
    for off in (0, blk):
        dist = off + qry - key
        valid = (dist >= 0) & (dist < SWA_WINDOW)
        tabs.append(np.concatenate([np.where(valid, -s * LOG2E * dist, NEG_BIG) for s in slopes], axis=1))
    return jnp.asarray(np.stack(tabs), jnp.float32)


def _swa_attention(sinks, q, k, vt, bias, mix_d, batch, seq):
    tq = 4096
    nq = seq // tq
    blk = SWA_WINDOW
    return pl.pallas_call(
        _swa_kernel,
        grid=(batch, nq),
        in_specs=[pl.BlockSpec(memory_space=pltpu.SMEM),
                  pl.BlockSpec((tq, D_GROUP), lambda b, i: (b * nq + i, 0)),
                  pl.BlockSpec((seq, D_GROUP), lambda b, i: (b, 0)),
                  pl.BlockSpec((None, SLOT, seq), lambda b, i: (b, 0, 0)),
                  pl.BlockSpec((2, 2 * blk, SWA_HEADS * blk), lambda b, i: (0, 0, 0)),
                  pl.BlockSpec((1, D_GROUP), lambda b, i: (0, 0))],
        out_specs=pl.BlockSpec((tq, D_GROUP), lambda b, i: (b * nq + i, 0)),
        out_shape=jax.ShapeDtypeStruct((batch * seq, D_GROUP), jnp.bfloat16),
        scratch_shapes=[pltpu.VMEM((D_GROUP, tq), jnp.float32)],
        compiler_params=pltpu.CompilerParams(dimension_semantics=("arbitrary", "arbitrary"),
                                             vmem_limit_bytes=VMEM_LIMIT_ATTENTION),
        name="swa_attention",
    )(sinks, q, k, vt, bias, mix_d)


def _out_stage_kernel(x_ref, ya_ref, yb_ref, yc_ref, yd_ref, w_o_ref, g_ffn_ref, w_gu_ref,
                      w_down_ref, g_final_ref, o_ref, act_scr, *, final):
    mixed = jnp.concatenate([ya_ref[...], yb_ref[...], yc_ref[...], yd_ref[...]], axis=1)
    x = x_ref[...] + _dot(mixed, w_o_ref[...])
    h2 = _rms(x, g_ffn_ref[...]).astype(jnp.bfloat16)

    for c in range(N_FF_CHUNKS):
        cols = slice(c * FF_CHUNK, (c + 1) * FF_CHUNK)
        gate = _dot(h2, w_gu_ref[:, cols])
        up = _dot(h2, w_gu_ref[:, D_FF + c * FF_CHUNK:D_FF + (c + 1) * FF_CHUNK])
        act_scr[:, cols] = (gate * jax.nn.sigmoid(gate) * up).astype(jnp.bfloat16)
    x = x + _dot(act_scr[...], w_down_ref[...])
    if final:
        x = _rms(x, g_final_ref[...])
    o_ref[...] = x


def _out_stage(x2d, ya, yb, yc, yd, lw, stacked, l, g_final, final):
    n_tok = x2d.shape[0]
    tm = TM_OUT
    const = lambda shape: pl.BlockSpec(shape, lambda t: (0,) * len(shape), pipeline_mode=pl.Buffered(1))
    of_layer = lambda shape: pl.BlockSpec((None,) + shape, lambda t: (l,) + (0,) * len(shape),
                                          pipeline_mode=pl.Buffered(1))
    tile = lambda w: pl.BlockSpec((tm, w), lambda t: (t, 0))
    return pl.pallas_call(
        functools.partial(_out_stage_kernel, final=final),
        grid=(n_tok // tm,),
        in_specs=[tile(D_MODEL), tile(D_GROUP), tile(D_GROUP), tile(D_GROUP), tile(D_GROUP),
                  of_layer((D_MODEL, D_MODEL)), const((1, D_MODEL)), of_layer((D_MODEL, 2 * D_FF)),
                  of_layer((D_FF, D_MODEL)), const((1, D_MODEL))],
        out_specs=tile(D_MODEL),
        out_shape=jax.ShapeDtypeStruct((n_tok, D_MODEL), jnp.float32),
        scratch_shapes=[pltpu.VMEM((tm, D_FF), jnp.bfloat16)],
        compiler_params=pltpu.CompilerParams(dimension_semantics=("arbitrary",),
                                             vmem_limit_bytes=VMEM_LIMIT_OUT_STAGE),
        name="out_stage",
    )(x2d, ya, yb, yc, yd, stacked["w_o"], lw["g_ffn"], stacked["w_gate_up"], stacked["w_down"], g_final)


def _rope_tables(seq):
    inv = 1.0 / (ROPE_THETA ** (jnp.arange(0, MLA_ROPE, 2, dtype=jnp.float32) / MLA_ROPE))
    ang = jnp.arange(seq, dtype=jnp.float32)[:, None] * inv[None, :]
    cos, sin = jnp.cos(ang), jnp.sin(ang)
    cos2 = jnp.concatenate([cos, cos], axis=1)
    sin2 = jnp.concatenate([sin, sin], axis=1)
    scale = LOG2E / math.sqrt(MLA_NOPE + MLA_ROPE)
    tq = jnp.concatenate([jnp.full((seq, MLA_NOPE), scale, jnp.float32), cos2 * scale, sin2 * scale], axis=1)
    tk = jnp.concatenate([jnp.zeros((seq, MLA_NOPE), jnp.float32), cos2, sin2], axis=1)
    return {"tq": tq, "tk": tk}


def _swap_halves(w):
    half = w.shape[-1] // 2
    return jnp.concatenate([-w[..., half:], w[..., :half]], axis=-1)


def _layer_weights(l, attn_norm, w_in, mla_q_norm, w_uq, mla_kv_norm, w_ukv, conv_w, pool_w, pool_scale,
                   mix_norm, ffn_norm):
    bf = jnp.bfloat16
    f32 = jnp.float32
    wi = w_in[l]
    pts = np.cumsum((0, 256, 128, 32, 256, 256, 256, 256, 256, 128, 128))
    c_q, c_kv, k_r, g_b, g_c, u_conv, u_pool, q_sw, k_sw, v_sw = [wi[:, pts[j]:pts[j + 1]] for j in range(10)]
    zeros = lambda w: jnp.zeros((D_MODEL, w), f32)
    w_in_r = jnp.concatenate([c_q, c_kv, zeros(MLA_NOPE), k_r, _swap_halves(k_r),
                              g_b, g_c, u_conv, u_pool, q_sw, k_sw, v_sw], axis=1)
    wq = w_uq[l].reshape(MLA_Q_RANK, MLA_HEADS, MLA_NOPE + MLA_ROPE)
    wq_rot = wq[..., MLA_NOPE:]
    w_uq_p = jnp.concatenate([wq, _swap_halves(wq_rot)], axis=-1).reshape(MLA_Q_RANK, MLA_W)
    wkv = w_ukv[l].reshape(MLA_KV_RANK, MLA_HEADS, MLA_NOPE + MLA_V)
    zk = jnp.zeros((MLA_KV_RANK, MLA_HEADS, SLOT - MLA_NOPE), f32)
    w_k = jnp.concatenate([wkv[..., :MLA_NOPE], zk], axis=-1).reshape(MLA_KV_RANK, MLA_W)
    zv = jnp.zeros((MLA_KV_RANK, MLA_HEADS, VT_ROWS - MLA_V), f32)
    w_v = jnp.concatenate([wkv[..., MLA_NOPE:], zv], axis=-1).reshape(MLA_KV_RANK, VT_ALL)
    w_v = jnp.concatenate([w_v, jnp.zeros((MLA_KV_RANK, VT_PAD - VT_ALL), f32)], axis=1)
    w_pool = jax.scipy.linalg.block_diag(*[pool_w[l, g] for g in range(len(POOL_WINDOWS))])
    mix = mix_norm[l].reshape(4, 1, D_GROUP)
    return {
        "g_attn": attn_norm[l][None, :], "w_in": w_in_r.astype(bf),
        "g_q": mla_q_norm[l][None, :], "w_uq": w_uq_p.astype(bf),
        "g_kv": mla_kv_norm[l][None, :], "w_kv": jnp.concatenate([w_k, w_v], axis=1).astype(bf),
        "conv_w": conv_w[l], "w_pool": w_pool.astype(bf), "pool_scale": pool_scale[l][None, :],
        "mix_a": mix[0], "mix_b": mix[1], "mix_c": mix[2], "mix_d": mix[3],
        "g_ffn": ffn_norm[l][None, :],
    }


def kernel(x, attn_norm, w_in, mla_q_norm, w_uq, mla_kv_norm, w_ukv, conv_w, pool_w, pool_scale, swa_sinks,
           mix_norm, w_o, ffn_norm, w_gate_up, w_down, final_norm):
    batch, seq, d_model = x.shape
    depth = w_in.shape[0]
    assert d_model == D_MODEL and w_in.shape[2] == D_IN
    assert seq % TM_IN == 0 and seq % TQ == 0 and TQ % TK == 0 and (batch * seq) % TM_OUT == 0
    slopes = tuple(float(2.0 ** (-8.0 * (h + 1) / SWA_HEADS)) for h in range(SWA_HEADS))
    tabs = _rope_tables(seq)
    swa_bias = _swa_bias_tables(slopes)
    x2d = x.reshape(batch * seq, D_MODEL)
    g_final = final_norm[None, :]
    stacked = {"w_o": w_o.astype(jnp.bfloat16), "w_gate_up": w_gate_up.astype(jnp.bfloat16),
               "w_down": w_down.astype(jnp.bfloat16)}
    for l in range(depth):
        lw = _layer_weights(l, attn_norm, w_in, mla_q_norm, w_uq, mla_kv_norm, w_ukv, conv_w, pool_w,
                            pool_scale, mix_norm, ffn_norm)
        q, k, vt, yb, yc, qsw, ksw, vswt = _in_stage(x2d, lw, tabs, seq)
        ya = _mla_attention(q, k, vt, lw["mix_a"], batch, seq)
        yd = _swa_attention(swa_sinks[l], qsw, ksw, vswt, swa_bias, lw["mix_d"], batch, seq)
        x2d = _out_stage(x2d, ya, yb, yc, yd, lw, stacked, l, g_final, final=(l == depth - 1))
    return x2d.reshape(batch, seq, D_MODEL)
```
